```python
import math
import jax, jax.numpy as jnp
from jax import lax
import numpy as np

D_MODEL = 1024
BATCH = 16
SEQ = 4096
DEPTH = 1

PLE_DIM = 256
D_FF = 2816

GDN_HEADS = 4
GDN_DK = 128
GDN_DV = 128
GDN_CONV = 4
GDN_CHUNK = 64

DIFF_HEADS = 4
DIFF_DQK = 64
DIFF_DV = 2 * DIFF_DQK
ROPE_THETA = 500000.0
ROPE_DIM = DIFF_DQK // 4
Q_BLOCK = 128

LN_EPS = 1e-5
RMS_EPS = 1e-6

GDN_QK_W = GDN_HEADS * GDN_DK
GDN_V_W = GDN_HEADS * GDN_DV
DIFF_QK_W = DIFF_HEADS * 2 * DIFF_DQK
DIFF_V_W = DIFF_HEADS * DIFF_DV
MIX_WIDTH = GDN_V_W + DIFF_V_W
IN_SECTIONS = (GDN_QK_W, GDN_QK_W, GDN_V_W, GDN_V_W, GDN_HEADS, GDN_HEADS,
               DIFF_QK_W, DIFF_QK_W, DIFF_V_W)
IN_WIDTH = sum(IN_SECTIONS)

DEEPNORM_ALPHA = (2.0 * DEPTH) ** 0.25
DEEPNORM_BETA = (8.0 * DEPTH) ** -0.25

kernel_name = "hybrid_gdn_diffattn_macaron_deepnorm"


def layer_norm(x, g, b):
    xf = x.astype(jnp.float32)
    mu = jnp.mean(xf, axis=-1, keepdims=True)
    var = jnp.mean(jnp.square(xf - mu), axis=-1, keepdims=True)
    return ((xf - mu) * lax.rsqrt(var + LN_EPS) * g.astype(jnp.float32)
            + b.astype(jnp.float32)).astype(x.dtype)


def rms_norm(x, w):
    xf = x.astype(jnp.float32)
    y = xf * lax.rsqrt(jnp.mean(xf * xf, axis=-1, keepdims=True) + RMS_EPS)
    return (y * w.astype(jnp.float32)).astype(x.dtype)


def l2_normalize(x):
    xf = x.astype(jnp.float32)
    return xf * lax.rsqrt(jnp.sum(xf * xf, axis=-1, keepdims=True) + RMS_EPS)


def swiglu_ffn(x, w13, w2):
    gate, up = jnp.split(x @ w13, 2, axis=-1)
    return (jax.nn.silu(gate) * up) @ w2


def rotary_tables(seq_len):
    inv_freq = ROPE_THETA ** (-jnp.arange(0, ROPE_DIM, 2, dtype=jnp.float32) / ROPE_DIM)
    ang = jnp.arange(seq_len, dtype=jnp.float32)[:, None] * inv_freq[None, :]
    return jnp.cos(ang), jnp.sin(ang)


def partial_rotary(x, cos, sin):
    half = ROPE_DIM // 2
    x1, x2, rest = x[..., :half], x[..., half:ROPE_DIM], x[..., ROPE_DIM:]
    c = cos[:, None, None, :]
    s = sin[:, None, None, :]
    rot = jnp.concatenate([x1 * c - x2 * s, x2 * c + x1 * s], axis=-1).astype(x.dtype)
    return jnp.concatenate([rot, rest], axis=-1)


def causal_depthwise_conv(x, w):
    k_width, ch = w.shape
    return lax.conv_general_dilated(
        x, w[:, None, :].astype(x.dtype), window_strides=(1,), padding=[(k_width - 1, 0)],
        dimension_numbers=("NWC", "WIO", "NWC"), feature_group_count=ch)


def gated_delta_rule_chunked(q, k, v, g, beta):
    bsz, seq, nh, dk = q.shape
    dv = v.shape[-1]
    c = GDN_CHUNK
    n = seq // c

    def chunks(t):
        t = t.reshape((bsz, n, c, nh) + t.shape[3:])
        return jnp.moveaxis(t, 3, 1)

    q = chunks(q) * (dk ** -0.5)
    k = chunks(k)
    v = chunks(v)
    beta = chunks(beta)
    g = jnp.cumsum(chunks(g), axis=-1)

    causal = jnp.tril(jnp.ones((c, c), dtype=bool))
    strict = jnp.tril(jnp.ones((c, c), dtype=bool), -1)
    decay = jnp.exp(jnp.where(causal, g[..., :, None] - g[..., None, :], -jnp.inf))

    kk = jnp.einsum("bhncd,bhnsd->bhncs", k, k)
    lower = jnp.where(strict, kk * decay * beta[..., None], 0.0)
    eye = jnp.eye(c, dtype=jnp.float32)
    t_mat = lax.linalg.triangular_solve(eye + lower, jnp.broadcast_to(eye, lower.shape),
                                        left_side=True, lower=True, unit_diagonal=True)
    u = jnp.einsum("bhncs,bhnsv->bhncv", t_mat, v * beta[..., None])
    w = jnp.einsum("bhncs,bhnsd->bhncd", t_mat, k * (beta * jnp.exp(g))[..., None])

    qk = jnp.einsum("bhncd,bhnsd->bhncs", q, k) * decay
    q_dec = q * jnp.exp(g)[..., None]
    g_last = g[..., -1]
    k_dec = k * jnp.exp(g_last[..., None] - g)[..., None]

    def step(state, xs):
        u_n, w_n, qk_n, qdec_n, kdec_n, glast_n = xs
        v_new = u_n - jnp.einsum("bhck,bhkv->bhcv", w_n, state)
        o = (jnp.einsum("bhck,bhkv->bhcv", qdec_n, state)
             + jnp.einsum("bhcs,bhsv->bhcv", qk_n, v_new))
        state = (state * jnp.exp(glast_n)[..., None, None]
                 + jnp.einsum("bhck,bhcv->bhkv", kdec_n, v_new))
        return state, o

    xs = tuple(jnp.moveaxis(t, 2, 0) for t in (u, w, qk, q_dec, k_dec, g_last))
    state0 = jnp.zeros((bsz, nh, dk, dv), dtype=jnp.float32)
    _, o = lax.scan(step, state0, xs)
    o = jnp.moveaxis(o, 0, 2)
    return jnp.moveaxis(o, 1, 3).reshape(bsz, seq, nh, dv)


def gdn_mixer(q, k, v, z, a, b, conv_w, a_log, dt_bias, norm_w):
    bsz, seq, _ = q.shape
    qkv = jax.nn.silu(causal_depthwise_conv(jnp.concatenate([q, k, v], axis=-1), conv_w))
    q, k, v = jnp.split(qkv, [GDN_QK_W, 2 * GDN_QK_W], axis=-1)
    q = l2_normalize(q.reshape(bsz, seq, GDN_HEADS, GDN_DK))
    k = l2_normalize(k.reshape(bsz, seq, GDN_HEADS, GDN_DK))
    v = v.reshape(bsz, seq, GDN_HEADS, GDN_DV).astype(jnp.float32)
    g = -jnp.exp(a_log.astype(jnp.float32)) * jax.nn.softplus(
        a.astype(jnp.float32) + dt_bias.astype(jnp.float32))
    beta = jax.nn.sigmoid(b.astype(jnp.float32))
    o = gated_delta_rule_chunked(q, k, v, g, beta)
    gate = jax.nn.silu(z.reshape(bsz, seq, GDN_HEADS, GDN_DV).astype(jnp.float32))
    o = rms_norm(o, norm_w) * gate
    return o.reshape(bsz, seq, GDN_V_W).astype(z.dtype)


def diff_attention_mixer(q, k, v, lq1, lk1, lq2, lk2, subln_w, lam_init, cos, sin):
    bsz, seq, _ = q.shape
    q = partial_rotary(q.reshape(bsz, seq, DIFF_HEADS, 2, DIFF_DQK), cos, sin)
    k = partial_rotary(k.reshape(bsz, seq, DIFF_HEADS, 2, DIFF_DQK), cos, sin)
    v = v.reshape(bsz, seq, DIFF_HEADS, DIFF_DV)
    f32 = jnp.float32
    lam = (jnp.exp(jnp.sum(lq1.astype(f32) * lk1.astype(f32)))
           - jnp.exp(jnp.sum(lq2.astype(f32) * lk2.astype(f32))) + lam_init)
    nb = seq // Q_BLOCK
    q_blocks = jnp.moveaxis(q.reshape(bsz, nb, Q_BLOCK, DIFF_HEADS, 2, DIFF_DQK), 1, 0)
    pos = jnp.arange(seq, dtype=jnp.int32)
    q_pos = pos.reshape(nb, Q_BLOCK)
    scale = DIFF_DQK ** -0.5

    def block(args):
        qb, qp = args
        s = jnp.einsum("bqhmd,bkhmd->bhmqk", qb, k).astype(f32) * scale
        s = jnp.where(qp[:, None] >= pos[None, :], s, -jnp.inf)
        pr = jax.nn.softmax(s, axis=-1)
        attn = pr[:, :, 0] - lam * pr[:, :, 1]
        return jnp.einsum("bhqk,bkhd->bqhd", attn.astype(v.dtype), v)

    o = lax.map(block, (q_blocks, q_pos))
    o = jnp.moveaxis(o, 0, 1).reshape(bsz, seq, DIFF_HEADS, DIFF_DV)
    o = rms_norm(o, subln_w) * (1.0 - lam_init)
    return o.reshape(bsz, seq, DIFF_V_W)


def setup_inputs(seed: int = 0) -> dict:
    key = jax.random.key(seed)
    ks = jax.random.split(key, 32)
    L, D, F = DEPTH, D_MODEL, D_FF
    nrm = lambda k, shape, s: jax.random.normal(k, shape, dtype=jnp.float32) * s
    gain = lambda k, shape: 1.0 + 0.02 * jax.random.normal(k, shape, dtype=jnp.float32)
    dt = jnp.exp(jax.random.uniform(ks[14], (L, GDN_HEADS), minval=math.log(1e-3), maxval=math.log(1e-1)))
    return {
        "x": nrm(ks[0], (BATCH, SEQ, D), 1.0),
        "p": nrm(ks[1], (DEPTH, BATCH, SEQ, PLE_DIM), 1.0),
        "ffn1_w13": nrm(ks[2], (L, D, 2 * F), D ** -0.5),
        "ffn1_w2": nrm(ks[3], (L, F, D), F ** -0.5 * DEEPNORM_BETA),
        "ln1_g": gain(ks[4], (L, D)),
        "ln1_b": nrm(ks[5], (L, D), 0.02),
        "w_in": nrm(ks[6], (L, D, IN_WIDTH), D ** -0.5),
        "gdn_conv_w": nrm(ks[7], (L, GDN_CONV, 2 * GDN_QK_W + GDN_V_W), GDN_CONV ** -0.5),
        "gdn_a_log": jnp.log(jax.random.uniform(ks[8], (L, GDN_HEADS), minval=1.0, maxval=16.0)),
        "gdn_dt_bias": dt + jnp.log(-jnp.expm1(-dt)),
        "gdn_norm_w": gain(ks[9], (L, GDN_DV)),
        "diff_lq1": nrm(ks[10], (L, DIFF_DQK), 0.1),
        "diff_lk1": nrm(ks[11], (L, DIFF_DQK), 0.1),
        "diff_lq2": nrm(ks[12], (L, DIFF_DQK), 0.1),
        "diff_lk2": nrm(ks[13], (L, DIFF_DQK), 0.1),
        "diff_subln_w": gain(ks[15], (L, DIFF_DV)),
        "w_out": nrm(ks[16], (L, MIX_WIDTH, D), MIX_WIDTH ** -0.5 * DEEPNORM_BETA),
        "ln2_g": gain(ks[17], (L, D)),
        "ln2_b": nrm(ks[18], (L, D), 0.02),
        "ffn2_w13": nrm(ks[19], (L, D, 2 * F), D ** -0.5),
        "ffn2_w2": nrm(ks[20], (L, F, D), F ** -0.5 * DEEPNORM_BETA),
        "ple_gate_w": nrm(ks[21], (L, D, D), D ** -0.5),
        "ple_proj_w": nrm(ks[22], (L, PLE_DIM, D), PLE_DIM ** -0.5 * DEEPNORM_BETA),
        "ln3_g": gain(ks[23], (L, D)),
        "ln3_b": nrm(ks[24], (L, D), 0.02),
    }


def reference(x, p, ffn1_w13, ffn1_w2, ln1_g, ln1_b, w_in, gdn_conv_w, gdn_a_log, gdn_dt_bias,
              gdn_norm_w, diff_lq1, diff_lk1, diff_lq2, diff_lk2, diff_subln_w, w_out, ln2_g, ln2_b,
              ffn2_w13, ffn2_w2, ple_gate_w, ple_proj_w, ln3_g, ln3_b):
    cos, sin = rotary_tables(x.shape[1])
    offsets = [int(o) for o in np.cumsum(IN_SECTIONS)[:-1]]
    alpha = DEEPNORM_ALPHA
    for i in range(DEPTH):
        lam_init = 0.8 - 0.6 * math.exp(-0.3 * i)
        x = layer_norm(alpha * x + 0.5 * swiglu_ffn(x, ffn1_w13[i], ffn1_w2[i]), ln1_g[i], ln1_b[i])
        h = x @ w_in[i]
        gq, gk, gv, gz, ga, gb, dq, dk, dv = jnp.split(h, offsets, axis=-1)
        y_a = gdn_mixer(gq, gk, gv, gz, ga, gb, gdn_conv_w[i], gdn_a_log[i], gdn_dt_bias[i],
                        gdn_norm_w[i])
        y_b = diff_attention_mixer(dq, dk, dv, diff_lq1[i], diff_lk1[i], diff_lq2[i], diff_lk2[i],
                                   diff_subln_w[i], lam_init, cos, sin)
        mix = jnp.concatenate([y_a, y_b], axis=-1)
        x = layer_norm(alpha * x + mix @ w_out[i], ln2_g[i], ln2_b[i])
        ple = jax.nn.sigmoid(x @ ple_gate_w[i]) * (p[i] @ ple_proj_w[i])
        x = layer_norm(alpha * x + 0.5 * swiglu_ffn(x, ffn2_w13[i], ffn2_w2[i]) + ple,
                       ln3_g[i], ln3_b[i])
    return x
```

```python
import functools
import math

import jax
import jax.numpy as jnp
from jax import lax
from jax.experimental import pallas as pl
from jax.experimental.pallas import tpu as pltpu

F32 = jnp.float32
BF16 = jnp.bfloat16

LN_EPS = 1e-5
RMS_EPS = 1e-6
DEPTH = 1
DEEPNORM_ALPHA = (2.0 * DEPTH) ** 0.25
ROPE_THETA = 500000.0

GDN_HEADS = 4
GDN_DK = 128
GDN_DV = 128
GDN_CONV = 4
DIFF_HEADS = 4
DIFF_DQK = 64
DIFF_DV = 128
ROPE_DIM = DIFF_DQK // 4

VMEM_LIMIT_BYTES = 56 * 1024 * 1024
LANES = 128

FFN_ROWS = 256
PROJ_ROWS = 512
GDN_STEP = 512
GDN_CHUNK = 128
INV_BASE = 16
ATT_BLOCK = 512


def _const_spec(shape):
    nd = len(shape)
    return pl.BlockSpec(shape, lambda *_: (0,) * nd, pipeline_mode=pl.Buffered(1))


def _sigmoid(x):
    return 1.0 / (1.0 + jnp.exp(-x))


def _layer_norm(y, g, b):
    mu = jnp.mean(y, axis=-1, keepdims=True)
    d = y - mu
    var = jnp.mean(d * d, axis=-1, keepdims=True)
    return d * lax.rsqrt(var + LN_EPS) * g + b


def _dot(a, b):
    return jnp.dot(a, b, preferred_element_type=F32)


def _dot_nt(a, b):
    return lax.dot_general(a, b, (((1,), (1,)), ((), ())), preferred_element_type=F32)


def _dot_tn(a, b):
    return lax.dot_general(a, b, (((0,), (0,)), ((), ())), preferred_element_type=F32)


def _swiglu(xb, w13_ref, w2_ref, d_ff):
    h = _dot(xb, w13_ref[...])
    gate = h[:, :d_ff]
    up = h[:, d_ff:]
    act = (gate * _sigmoid(gate) * up).astype(BF16)
    return _dot(act, w2_ref[...])


def _ffn1_kernel(x_ref, w13_ref, w2_ref, g_ref, b_ref, o_ref, *, d_ff):
    x = x_ref[...]
    y = DEEPNORM_ALPHA * x + 0.5 * _swiglu(x.astype(BF16), w13_ref, w2_ref, d_ff)
    o_ref[...] = _layer_norm(y, g_ref[...], b_ref[...])


def _ffn2_kernel(x_ref, p_ref, w13_ref, w2_ref, wg_ref, wp_ref, g_ref, b_ref, o_ref, *, d_ff):
    x = x_ref[...]
    xb = x.astype(BF16)
    ple = _sigmoid(_dot(xb, wg_ref[...])) * _dot(p_ref[...].astype(BF16), wp_ref[...])
    y = DEEPNORM_ALPHA * x + 0.5 * _swiglu(xb, w13_ref, w2_ref, d_ff) + ple
    o_ref[...] = _layer_norm(y, g_ref[...], b_ref[...])


def _ffn1_call(x2d, w13, w2, g, b):
    n, d = x2d.shape
    d_ff = w2.shape[0]
    rows = min(FFN_ROWS, n)
    row_spec = pl.BlockSpec((rows, d), lambda i: (i, 0))
    return pl.pallas_call(
        functools.partial(_ffn1_kernel, d_ff=d_ff),
        grid=(n // rows,),
        in_specs=[row_spec, _const_spec(w13.shape), _const_spec(w2.shape),
                  _const_spec(g.shape), _const_spec(b.shape)],
        out_specs=row_spec,
        out_shape=jax.ShapeDtypeStruct((n, d), F32),
        compiler_params=pltpu.CompilerParams(
            dimension_semantics=("parallel",), vmem_limit_bytes=VMEM_LIMIT_BYTES),
        name="ffn1_ln1",
    )(x2d, w13, w2, g, b)


def _ffn2_call(x2d, p2d, w13, w2, wg, wp, g, b):
    n, d = x2d.shape
    d_ff = w2.shape[0]
    rows = min(FFN_ROWS, n)
    row_spec = pl.BlockSpec((rows, d), lambda i: (i, 0))
    p_spec = pl.BlockSpec((rows, p2d.shape[1]), lambda i: (i, 0))
    return pl.pallas_call(
        functools.partial(_ffn2_kernel, d_ff=d_ff),
        grid=(n // rows,),
        in_specs=[row_spec, p_spec, _const_spec(w13.shape), _const_spec(w2.shape),
                  _const_spec(wg.shape), _const_spec(wp.shape),
                  _const_spec(g.shape), _const_spec(b.shape)],
        out_specs=row_spec,
        out_shape=jax.ShapeDtypeStruct((n, d), F32),
        compiler_params=pltpu.CompilerParams(
            dimension_semantics=("parallel",), vmem_limit_bytes=VMEM_LIMIT_BYTES),
        name="ffn2_ple_ln3",
    )(x2d, p2d, w13, w2, wg, wp, g, b)


def _rotary(x, cos_t, sin_lo, sin_hi):
    half = ROPE_DIM // 2
    from_hi = pltpu.roll(x, LANES - half, axis=1)
    from_lo = pltpu.roll(x, half, axis=1)
    return x * cos_t + from_hi * sin_lo + from_lo * sin_hi


def _inproj_kernel(x_ref, w_ref, cos_ref, slo_ref, shi_ref,
                   qkv_ref, z_ref, ab_ref, dq_ref, dk_ref, dv_ref, *, q_scale):
    h = _dot(x_ref[...].astype(BF16), w_ref[...])
    gw = GDN_HEADS * GDN_DK
    qkv_ref[...] = h[:, :3 * gw]
    z_ref[...] = h[:, 3 * gw:4 * gw].astype(BF16)
    ab_ref[...] = h[:, 4 * gw:4 * gw + LANES]
    base = 4 * gw + LANES
    dw = DIFF_HEADS * 2 * DIFF_DQK
    cos_t, sin_lo, sin_hi = cos_ref[...], slo_ref[...], shi_ref[...]
    for hd in range(DIFF_HEADS):
        lo = hd * LANES
        q = _rotary(h[:, base + lo:base + lo + LANES], cos_t, sin_lo, sin_hi)
        dq_ref[:, lo:lo + LANES] = (q * q_scale).astype(BF16)
        k = _rotary(h[:, base + dw + lo:base + dw + lo + LANES], cos_t, sin_lo, sin_hi)
        dk_ref[:, lo:lo + LANES] = k.astype(BF16)
    dv_ref[...] = h[:, base + 2 * dw:base + 3 * dw].astype(BF16)


def _rotary_tables(seq):
    half = ROPE_DIM // 2
    inv_freq = ROPE_THETA ** (-jnp.arange(0, ROPE_DIM, 2, dtype=F32) / ROPE_DIM)
    ang = jnp.arange(seq, dtype=F32)[:, None] * inv_freq[None, :]
    cos, sin = jnp.cos(ang), jnp.sin(ang)
    ones = jnp.ones((seq, DIFF_DQK - ROPE_DIM), F32)
    zeros = jnp.zeros((seq, DIFF_DQK - ROPE_DIM), F32)
    zh = jnp.zeros((seq, half), F32)
    cos_map = jnp.concatenate([cos, cos, ones], axis=1)
    lo_map = jnp.concatenate([-sin, zh, zeros], axis=1)
    hi_map = jnp.concatenate([zh, sin, zeros], axis=1)
    tile2 = lambda t: jnp.concatenate([t, t], axis=1)
    return tile2(cos_map), tile2(lo_map), tile2(hi_map)


def _inproj_call(x1, w_cat, seq):
    n, d = x1.shape
    rows = min(PROJ_ROWS, seq)
    blocks_per_seq = seq // rows
    cos_t, sin_lo, sin_hi = _rotary_tables(seq)
    gw = GDN_HEADS * GDN_DK
    dw = DIFF_HEADS * 2 * DIFF_DQK
    dvw = DIFF_HEADS * DIFF_DV
    q_scale = DIFF_DQK ** -0.5 * math.log2(math.e)
    row = lambda w: pl.BlockSpec((rows, w), lambda i: (i, 0))
    tab = pl.BlockSpec((rows, LANES), lambda i: (i % blocks_per_seq, 0))
    return pl.pallas_call(
        functools.partial(_inproj_kernel, q_scale=q_scale),
        grid=(n // rows,),
        in_specs=[row(d), _const_spec(w_cat.shape), tab, tab, tab],
        out_specs=[row(3 * gw), row(gw), row(LANES), row(dw), row(dw), row(dvw)],
        out_shape=[jax.ShapeDtypeStruct((n, 3 * gw), F32),
                   jax.ShapeDtypeStruct((n, gw), BF16),
                   jax.ShapeDtypeStruct((n, LANES), F32),
                   jax.ShapeDtypeStruct((n, dw), BF16),
                   jax.ShapeDtypeStruct((n, dw), BF16),
                   jax.ShapeDtypeStruct((n, dvw), BF16)],
        compiler_params=pltpu.CompilerParams(
            dimension_semantics=("parallel",), vmem_limit_bytes=VMEM_LIMIT_BYTES),
        name="in_proj",
    )(x1, w_cat, cos_t, sin_lo, sin_hi)


def _segment_cumsum(x, seg, reverse=False):
    width = x.shape[1]
    pos = lax.broadcasted_iota(jnp.int32, x.shape, 1) % seg
    step = 1
    while step < seg:
        if reverse:
            shifted = pltpu.roll(x, width - step, axis=1)
            keep = pos < seg - step
        else:
            shifted = pltpu.roll(x, step, axis=1)
            keep = pos >= step
        x = x + jnp.where(keep, shifted, 0.0)
        step *= 2
    return x


def _unit_lower_inverse(low, ci, cj):
    c = low.shape[0]
    base = min(INV_BASE, c)
    shift = base.bit_length() - 1
    diag = jnp.where((ci >> shift) == (cj >> shift), low, 0.0)
    inv = (ci == cj).astype(F32) - diag
    db = diag.astype(BF16)
    power = _dot(db, db)
    span = 2
    while True:
        pb = power.astype(BF16)
        inv = inv + _dot(inv.astype(BF16), pb)
        span *= 2
        if span >= base:
            break
        power = _dot(pb, pb)
    size = base
    while size < c:
        shift = size.bit_length() - 1
        off = jnp.where(((ci >> shift) ^ (cj >> shift)) == 1, low, 0.0)
        ib = inv.astype(BF16)
        inv = inv - _dot(_dot(ib, off.astype(BF16)).astype(BF16), ib)
        size *= 2
    return inv


def _gdn_kernel(qkv_ref, z_ref, ab_ref, convw_ref, alog_ref, dtb_ref, normw_ref, y_ref,
                xin_ref, state_ref, *, chunk):
    step_len = qkv_ref.shape[1]
    nh = GDN_HEADS
    gw = nh * GDN_DK
    halo = 8

    @pl.when(pl.program_id(1) == 0)
    def _():
        xin_ref[0:halo, :] = jnp.zeros((halo, xin_ref.shape[1]), F32)
        state_ref[...] = jnp.zeros(state_ref.shape, F32)

    xin_ref[halo:halo + step_len, :] = qkv_ref[0]

    conv = None
    for j in range(GDN_CONV):
        start = halo - (GDN_CONV - 1) + j
        term = convw_ref[j:j + 1, :] * xin_ref[start:start + step_len, :]
        conv = term if conv is None else conv + term
    xin_ref[0:halo, :] = xin_ref[step_len:step_len + halo, :]
    act = conv * _sigmoid(conv)

    ab_t = ab_ref[0].T[0:2 * nh, :]
    row = lax.broadcasted_iota(jnp.int32, ab_t.shape, 0)
    sp_in = ab_t + dtb_ref[...]
    softplus = jnp.maximum(sp_in, 0.0) + jnp.log(1.0 + jnp.exp(-jnp.abs(sp_in)))
    g_t = jnp.where(row < nh, -jnp.exp(alog_ref[...]) * softplus, 0.0)
    beta_t = pltpu.roll(_sigmoid(ab_t), nh, axis=0)
    gc_t = _segment_cumsum(g_t, chunk)
    g_rev = _segment_cumsum(g_t, chunk, reverse=True)
    tail_t = g_rev - g_t
    glast_t = gc_t + tail_t
    egc_t = jnp.exp(gc_t)
    rows_t = jnp.concatenate(
        [gc_t, egc_t, jnp.exp(tail_t), beta_t, jnp.zeros((LANES - 4 * 2 * nh, step_len), F32)], axis=0)
    cols = rows_t.T
    bg_t = beta_t * egc_t
    eglast_t = jnp.exp(glast_t)

    ci = lax.broadcasted_iota(jnp.int32, (chunk, chunk), 0)
    cj = lax.broadcasted_iota(jnp.int32, (chunk, chunk), 1)
    normw = normw_ref[...]

    for hd in range(nh):
        lo = hd * GDN_DK
        q = act[:, lo:lo + GDN_DK]
        k = act[:, gw + lo:gw + lo + GDN_DK]
        v = act[:, 2 * gw + lo:2 * gw + lo + GDN_DV]
        q = q * (lax.rsqrt(jnp.sum(q * q, axis=-1, keepdims=True) + RMS_EPS) * GDN_DK ** -0.5)
        k = k * lax.rsqrt(jnp.sum(k * k, axis=-1, keepdims=True) + RMS_EPS)
        outs = []
        for c in range(step_len // chunk):
            r0 = c * chunk
            qc, kc, vc = q[r0:r0 + chunk], k[r0:r0 + chunk], v[r0:r0 + chunk]
            col = lambda quantity: cols[r0:r0 + chunk, 8 * quantity + hd:8 * quantity + hd + 1]
            rowv = lambda t: t[hd:hd + 1, r0:r0 + chunk]
            decay = jnp.where(ci >= cj, jnp.exp(jnp.minimum(col(0) - rowv(gc_t), 0.0)), 0.0)
            kb = kc.astype(BF16)
            kk = _dot_nt(kb, kb)
            qk = _dot_nt(qc.astype(BF16), kb) * decay
            low = jnp.where(ci > cj, kk * decay * col(3), 0.0)
            t_mat = _unit_lower_inverse(low, ci, cj)
            u = _dot((t_mat * rowv(beta_t)).astype(BF16), vc.astype(BF16))
            w = _dot((t_mat * rowv(bg_t)).astype(BF16), kb)
            q_dec = (qc * col(1)).astype(BF16)
            k_dec = (kc * col(2)).astype(BF16)
            state = state_ref[hd]
            sb = state.astype(BF16)
            v_new = u - _dot(w.astype(BF16), sb)
            vnb = v_new.astype(BF16)
            outs.append(_dot(q_dec, sb) + _dot(qk.astype(BF16), vnb))
            eg = eglast_t[hd:hd + 1, r0 + chunk - 1:r0 + chunk]
            state_ref[hd] = state * eg + _dot_tn(k_dec, vnb)
        o = jnp.concatenate(outs, axis=0)
        o = o * lax.rsqrt(jnp.mean(o * o, axis=-1, keepdims=True) + RMS_EPS) * normw
        z = z_ref[0, :, lo:lo + GDN_DV].astype(F32)
        y_ref[0, :, lo:lo + GDN_DV] = (o * (z * _sigmoid(z))).astype(y_ref.dtype)


def _gdn_call(qkv, z, ab, conv_w, a_log, dt_bias, norm_w):
    bsz, seq, width = qkv.shape
    nh = GDN_HEADS
    step_len = min(GDN_STEP, seq)
    chunk = min(GDN_CHUNK, step_len)
    pad = lambda t: jnp.concatenate([t.astype(F32), jnp.zeros((nh,), F32)])[:, None]
    blk = lambda w: pl.BlockSpec((1, step_len, w), lambda b, t: (b, t, 0))
    return pl.pallas_call(
        functools.partial(_gdn_kernel, chunk=chunk),
        grid=(bsz, seq // step_len),
        in_specs=[blk(width), blk(nh * GDN_DV), blk(LANES),
                  _const_spec(conv_w.shape), _const_spec((2 * nh, 1)), _const_spec((2 * nh, 1)),
                  _const_spec((1, GDN_DV))],
        out_specs=blk(nh * GDN_DV),
        out_shape=jax.ShapeDtypeStruct((bsz, seq, nh * GDN_DV), BF16),
        scratch_shapes=[pltpu.VMEM((step_len + 8, width), F32),
                        pltpu.VMEM((nh, GDN_DK, GDN_DV), F32)],
        compiler_params=pltpu.CompilerParams(
            dimension_semantics=("parallel", "arbitrary"), vmem_limit_bytes=VMEM_LIMIT_BYTES),
        name="gdn_mixer",
    )(qkv, z, ab, conv_w.astype(F32), pad(a_log), pad(dt_bias), norm_w.astype(F32)[None, :])


def _diffattn_kernel(q_ref, k_ref, v_ref, lq1_ref, lk1_ref, lq2_ref, lk2_ref, w_ref, o_ref,
                     acc1_ref, acc2_ref, *, lam_init):
    blk = q_ref.shape[1]
    qi = pl.program_id(2)
    q = q_ref[0]
    lane = lax.broadcasted_iota(jnp.int32, q.shape, 1)
    zero = jnp.zeros_like(q)
    q1 = jnp.where(lane < DIFF_DQK, q, zero)
    q2 = jnp.where(lane >= DIFF_DQK, q, zero)
    acc1_ref[...] = jnp.zeros(acc1_ref.shape, F32)
    acc2_ref[...] = jnp.zeros(acc2_ref.shape, F32)

    def block(j, carry, masked):
        m1, l1, m2, l2 = carry
        start = pl.multiple_of(j * blk, blk)
        k = k_ref[0, pl.ds(start, blk), :]
        v = v_ref[0, pl.ds(start, blk), :]
        new = []
        for qm, m, l, acc_ref in ((q1, m1, l1, acc1_ref), (q2, m2, l2, acc2_ref)):
            s = _dot_nt(k, qm)
            if masked:
                kv_pos = lax.broadcasted_iota(jnp.int32, s.shape, 0)
                q_pos = lax.broadcasted_iota(jnp.int32, s.shape, 1)
                s = jnp.where(q_pos >= kv_pos, s, -jnp.inf)
            m_new = jnp.maximum(m, jnp.max(s, axis=0, keepdims=True))
            scale = jnp.exp2(m - m_new)
            p = jnp.exp2(s - m_new)
            l_new = scale * l + jnp.sum(p, axis=0, keepdims=True)
            acc_ref[...] = acc_ref[...] * scale + _dot_tn(v, p.astype(BF16))
            new += [m_new, l_new]
        return tuple(new)

    init = (jnp.full((1, blk), -jnp.inf, F32), jnp.zeros((1, blk), F32),
            jnp.full((1, blk), -jnp.inf, F32), jnp.zeros((1, blk), F32))
    carry = lax.fori_loop(0, qi, functools.partial(block, masked=False), init)
    _, l1, _, l2 = block(qi, carry, masked=True)

    lam = (jnp.exp(jnp.sum(lq1_ref[...] * lk1_ref[...], keepdims=True))
           - jnp.exp(jnp.sum(lq2_ref[...] * lk2_ref[...], keepdims=True)) + lam_init)
    o = acc1_ref[...] / l1 - lam * (acc2_ref[...] / l2)
    o = o * lax.rsqrt(jnp.mean(o * o, axis=0, keepdims=True) + RMS_EPS)
    o_ref[0] = (o.T * (w_ref[...] * (1.0 - lam_init))).astype(o_ref.dtype)


def _diffattn_call(dq, dk, dv, lq1, lk1, lq2, lk2, subln_w, lam_init):
    bsz, seq, _ = dq.shape
    blk = min(ATT_BLOCK, seq)
    vec = lambda t: t.astype(F32)[None, :]
    q_spec = pl.BlockSpec((1, blk, LANES), lambda b, h, i: (b, i, h))
    kv_spec = pl.BlockSpec((1, seq, LANES), lambda b, h, i: (b, 0, h))
    return pl.pallas_call(
        functools.partial(_diffattn_kernel, lam_init=lam_init),
        grid=(bsz, DIFF_HEADS, seq // blk),
        in_specs=[q_spec, kv_spec, kv_spec,
                  _const_spec((1, DIFF_DQK)), _const_spec((1, DIFF_DQK)),
                  _const_spec((1, DIFF_DQK)), _const_spec((1, DIFF_DQK)),
                  _const_spec((1, DIFF_DV))],
        out_specs=q_spec,
        out_shape=jax.ShapeDtypeStruct((bsz, seq, DIFF_HEADS * DIFF_DV), BF16),
        scratch_shapes=[pltpu.VMEM((DIFF_DV, blk), F32), pltpu.VMEM((DIFF_DV, blk), F32)],
        compiler_params=pltpu.CompilerParams(
            dimension_semantics=("parallel", "parallel", "arbitrary"),
            vmem_limit_bytes=VMEM_LIMIT_BYTES),
        name="diff_attention",
    )(dq, dk, dv, vec(lq1), vec(lk1), vec(lq2), vec(lk2), vec(subln_w))


def _outproj_kernel(x_ref, ya_ref, yb_ref, wa_ref, wb_ref, g_ref, b_ref, o_ref):
    mix = _dot(ya_ref[...], wa_ref[...]) + _dot(yb_ref[...], wb_ref[...])
    o_ref[...] = _layer_norm(DEEPNORM_ALPHA * x_ref[...] + mix, g_ref[...], b_ref[...])


def _outproj_call(x1, ya, yb, wa, wb, g, b):
    n, d = x1.shape
    rows = min(PROJ_ROWS, n)
    row = lambda w: pl.BlockSpec((rows, w), lambda i: (i, 0))
    return pl.pallas_call(
        _outproj_kernel,
        grid=(n // rows,),
        in_specs=[row(d), row(ya.shape[1]), row(yb.shape[1]), _const_spec(wa.shape),
                  _const_spec(wb.shape), _const_spec(g.shape), _const_spec(b.shape)],
        out_specs=row(d),
        out_shape=jax.ShapeDtypeStruct((n, d), F32),
        compiler_params=pltpu.CompilerParams(
            dimension_semantics=("parallel",), vmem_limit_bytes=VMEM_LIMIT_BYTES),
        name="out_proj_ln2",
    )(x1, ya, yb, wa, wb, g, b)


def kernel(x, p, ffn1_w13, ffn1_w2, ln1_g, ln1_b, w_in, gdn_conv_w, gdn_a_log, gdn_dt_bias,
           gdn_norm_w, diff_lq1, diff_lk1, diff_lq2, diff_lk2, diff_subln_w, w_out, ln2_g, ln2_b,
           ffn2_w13, ffn2_w2, ple_gate_w, ple_proj_w, ln3_g, ln3_b):
    bsz, seq, d = x.shape
    n = bsz * seq
    gw = GDN_HEADS * GDN_DK
    n_ab = 2 * GDN_HEADS
    xs = x.reshape(n, d)
    for i in range(w_in.shape[0]):
        lam_init = 0.8 - 0.6 * math.exp(-0.3 * i)
        vec = lambda t: t[i].astype(F32)[None, :]
        xs = _ffn1_call(xs, ffn1_w13[i].astype(BF16), ffn1_w2[i].astype(BF16), vec(ln1_g), vec(ln1_b))
        w = w_in[i]
        w_cat = jnp.concatenate(
            [w[:, :4 * gw], w[:, 4 * gw:4 * gw + n_ab], jnp.zeros((d, LANES - n_ab), w.dtype),
             w[:, 4 * gw + n_ab:]], axis=1).astype(BF16)
        qkv, z, ab, dq, dk, dv = _inproj_call(xs, w_cat, seq)
        shape3 = lambda t: t.reshape(bsz, seq, t.shape[1])
        y_a = _gdn_call(shape3(qkv), shape3(z), shape3(ab), gdn_conv_w[i], gdn_a_log[i],
                        gdn_dt_bias[i], gdn_norm_w[i])
        y_b = _diffattn_call(shape3(dq), shape3(dk), shape3(dv), diff_lq1[i], diff_lk1[i],
                             diff_lq2[i], diff_lk2[i], diff_subln_w[i], lam_init)
        wo = w_out[i].astype(BF16)
        xs = _outproj_call(xs, y_a.reshape(n, -1), y_b.reshape(n, -1), wo[:gw], wo[gw:],
                           vec(ln2_g), vec(ln2_b))
        xs = _ffn2_call(xs, p[i].reshape(n, -1), ffn2_w13[i].astype(BF16), ffn2_w2[i].astype(BF16),
                        ple_gate_w[i].astype(BF16), ple_proj_w[i].astype(BF16),
                        vec(ln3_g), vec(ln3_b))
    return xs.reshape(bsz, seq, d)
```

```python
import functools
import math

import jax
import jax.numpy as jnp
from jax import lax
from jax.experimental import pallas as pl
from jax.experimental.pallas import tpu as pltpu

F32 = jnp.float32
BF16 = jnp.bfloat16

LN_EPS = 1e-5
RMS_EPS = 1e-6
DEPTH = 1
DEEPNORM_ALPHA = (2.0 * DEPTH) ** 0.25
ROPE_THETA = 500000.0

GDN_HEADS = 4
GDN_DK = 128
GDN_DV = 128
GDN_CONV = 4
DIFF_HEADS = 4
DIFF_DQK = 64
DIFF_DV = 128
ROPE_DIM = DIFF_DQK // 4

VMEM_LIMIT_BYTES = 56 * 1024 * 1024
LANES = 128

FFN_ROWS = 512
PROJ_ROWS = 512
GDN_STEP = 512
GDN_CHUNK = 128
INV_BASE = 16
ATT_BLOCK = 512


def _const_spec(shape):
    nd = len(shape)
    return pl.BlockSpec(shape, lambda *_: (0,) * nd, pipeline_mode=pl.Buffered(1))


def _sigmoid(x):
    return 1.0 / (1.0 + jnp.exp(-x))


def _layer_norm(y, g, b):
    mu = jnp.mean(y, axis=-1, keepdims=True)
    d = y - mu
    var = jnp.mean(d * d, axis=-1, keepdims=True)
    return d * lax.rsqrt(var + LN_EPS) * g + b


def _dot(a, b):
    return jnp.dot(a, b, preferred_element_type=F32)


def _dot_nt(a, b):
    return lax.dot_general(a, b, (((1,), (1,)), ((), ())), preferred_element_type=F32)


def _dot_tn(a, b):
    return lax.dot_general(a, b, (((0,), (0,)), ((), ())), preferred_element_type=F32)


def _swiglu(xb, w13_ref, w2_ref, d_ff):
    h = _dot(xb, w13_ref[...])
    gate = h[:, :d_ff]
    up = h[:, d_ff:]
    act = (gate * _sigmoid(gate) * up).astype(BF16)
    return _dot(act, w2_ref[...])


def _ffn1_kernel(x_ref, w13_ref, w2_ref, g_ref, b_ref, o_ref, *, d_ff):
    x = x_ref[...]
    y = DEEPNORM_ALPHA * x + 0.5 * _swiglu(x.astype(BF16), w13_ref, w2_ref, d_ff)
    o_ref[...] = _layer_norm(y, g_ref[...], b_ref[...])


def _ffn2_kernel(x1_ref, ya_ref, yb_ref, p_ref, wa_ref, wb_ref, g2_ref, b2_ref,
                 w13_ref, w2_ref, wg_ref, wp_ref, g_ref, b_ref, o_ref, *, d_ff):
    mix = _dot(ya_ref[...], wa_ref[...]) + _dot(yb_ref[...], wb_ref[...])
    x = _layer_norm(DEEPNORM_ALPHA * x1_ref[...] + mix, g2_ref[...], b2_ref[...])
    xb = x.astype(BF16)
    ple = _sigmoid(_dot(xb, wg_ref[...])) * _dot(p_ref[...].astype(BF16), wp_ref[...])
    y = DEEPNORM_ALPHA * x + 0.5 * _swiglu(xb, w13_ref, w2_ref, d_ff) + ple
    o_ref[...] = _layer_norm(y, g_ref[...], b_ref[...])


def _ffn1_call(x2d, w13, w2, g, b):
    n, d = x2d.shape
    d_ff = w2.shape[0]
    rows = min(FFN_ROWS, n)
    row_spec = pl.BlockSpec((rows, d), lambda i: (i, 0))
    return pl.pallas_call(
        functools.partial(_ffn1_kernel, d_ff=d_ff),
        grid=(n // rows,),
        in_specs=[row_spec, _const_spec(w13.shape), _const_spec(w2.shape),
                  _const_spec(g.shape), _const_spec(b.shape)],
        out_specs=row_spec,
        out_shape=jax.ShapeDtypeStruct((n, d), F32),
        compiler_params=pltpu.CompilerParams(
            dimension_semantics=("parallel",), vmem_limit_bytes=VMEM_LIMIT_BYTES),
        name="ffn1_ln1",
    )(x2d, w13, w2, g, b)


def _ffn2_call(x1, ya, yb, p2d, wa, wb, g2, b2, w13, w2, wg, wp, g, b):
    n, d = x1.shape
    d_ff = w2.shape[0]
    rows = min(FFN_ROWS, n)
    row = lambda w: pl.BlockSpec((rows, w), lambda i: (i, 0))
    consts = (wa, wb, g2, b2, w13, w2, wg, wp, g, b)
    return pl.pallas_call(
        functools.partial(_ffn2_kernel, d_ff=d_ff),
        grid=(n // rows,),
        in_specs=[row(d), row(ya.shape[1]), row(yb.shape[1]), row(p2d.shape[1])]
                 + [_const_spec(c.shape) for c in consts],
        out_specs=row(d),
        out_shape=jax.ShapeDtypeStruct((n, d), F32),
        compiler_params=pltpu.CompilerParams(
            dimension_semantics=("parallel",), vmem_limit_bytes=VMEM_LIMIT_BYTES),
        name="outproj_ln2_ffn2_ple_ln3",
    )(x1, ya, yb, p2d, *consts)


def _rotary(x, cos_t, sin_lo, sin_hi):
    half = ROPE_DIM // 2
    from_hi = pltpu.roll(x, LANES - half, axis=1)
    from_lo = pltpu.roll(x, half, axis=1)
    return x * cos_t + from_hi * sin_lo + from_lo * sin_hi


def _inproj_kernel(x_ref, w_ref, convw_ref, cos_ref, slo_ref, shi_ref,
                   gq_ref, gk_ref, gv_ref, z_ref, ab_ref, dq_ref, dk_ref, dv_ref, xin_ref,
                   *, q_scale, blocks_per_seq):
    rows = x_ref.shape[0]
    gw = GDN_HEADS * GDN_DK
    dw = DIFF_HEADS * 2 * DIFF_DQK
    base = 4 * gw + LANES
    halo = 8
    xb = x_ref[...].astype(BF16)
    proj = lambda lo, hi: _dot(xb, w_ref[:, lo:hi])

    @pl.when(lax.rem(pl.program_id(0), blocks_per_seq) == 0)
    def _():
        xin_ref[0:halo, :] = jnp.zeros((halo, xin_ref.shape[1]), F32)

    xin_ref[halo:halo + rows, :] = proj(0, 3 * gw)
    h_qk = proj(base, base + 2 * dw)
    conv = None
    for j in range(GDN_CONV):
        start = halo - (GDN_CONV - 1) + j
        term = convw_ref[j:j + 1, :] * xin_ref[start:start + rows, :]
        conv = term if conv is None else conv + term
    xin_ref[0:halo, :] = xin_ref[rows:rows + halo, :]
    act = conv * _sigmoid(conv)
    for hd in range(GDN_HEADS):
        lo = hd * GDN_DK
        q = act[:, lo:lo + GDN_DK]
        k = act[:, gw + lo:gw + lo + GDN_DK]
        q = q * (lax.rsqrt(jnp.sum(q * q, axis=-1, keepdims=True) + RMS_EPS) * GDN_DK ** -0.5)
        k = k * lax.rsqrt(jnp.sum(k * k, axis=-1, keepdims=True) + RMS_EPS)
        gq_ref[:, lo:lo + GDN_DK] = q.astype(BF16)
        gk_ref[:, lo:lo + GDN_DK] = k.astype(BF16)
    gv_ref[...] = act[:, 2 * gw:3 * gw].astype(BF16)
    h_rest = proj(3 * gw, base)
    cos_t, sin_lo, sin_hi = cos_ref[...], slo_ref[...], shi_ref[...]
    for hd in range(DIFF_HEADS):
        lo = hd * LANES
        q = _rotary(h_qk[:, lo:lo + LANES], cos_t, sin_lo, sin_hi)
        dq_ref[:, lo:lo + LANES] = (q * q_scale).astype(BF16)
        k = _rotary(h_qk[:, dw + lo:dw + lo + LANES], cos_t, sin_lo, sin_hi)
        dk_ref[:, lo:lo + LANES] = k.astype(BF16)
    z_ref[...] = h_rest[:, :gw].astype(BF16)
    ab_ref[...] = h_rest[:, gw:]
    dv_ref[...] = proj(base + 2 * dw, base + 3 * dw).astype(BF16)


def _rotary_tables(seq):
    half = ROPE_DIM // 2
    inv_freq = ROPE_THETA ** (-jnp.arange(0, ROPE_DIM, 2, dtype=F32) / ROPE_DIM)
    ang = jnp.arange(seq, dtype=F32)[:, None] * inv_freq[None, :]
    cos, sin = jnp.cos(ang), jnp.sin(ang)
    ones = jnp.ones((seq, DIFF_DQK - ROPE_DIM), F32)
    zeros = jnp.zeros((seq, DIFF_DQK - ROPE_DIM), F32)
    zh = jnp.zeros((seq, half), F32)
    cos_map = jnp.concatenate([cos, cos, ones], axis=1)
    lo_map = jnp.concatenate([-sin, zh, zeros], axis=1)
    hi_map = jnp.concatenate([zh, sin, zeros], axis=1)
    tile2 = lambda t: jnp.concatenate([t, t], axis=1)
    return tile2(cos_map), tile2(lo_map), tile2(hi_map)


def _inproj_call(x1, w_cat, conv_w, seq):
    n, d = x1.shape
    rows = min(PROJ_ROWS, seq)
    blocks_per_seq = seq // rows
    cos_t, sin_lo, sin_hi = _rotary_tables(seq)
    gw = GDN_HEADS * GDN_DK
    dw = DIFF_HEADS * 2 * DIFF_DQK
    dvw = DIFF_HEADS * DIFF_DV
    q_scale = DIFF_DQK ** -0.5 * math.log2(math.e)
    row = lambda w: pl.BlockSpec((rows, w), lambda i: (i, 0))
    tab = pl.BlockSpec((rows, LANES), lambda i: (i % blocks_per_seq, 0))
    bf = lambda w: jax.ShapeDtypeStruct((n, w), BF16)
    return pl.pallas_call(
        functools.partial(_inproj_kernel, q_scale=q_scale, blocks_per_seq=blocks_per_seq),
        grid=(n // rows,),
        in_specs=[row(d), _const_spec(w_cat.shape), _const_spec(conv_w.shape), tab, tab, tab],
        out_specs=[row(gw), row(gw), row(gw), row(gw), row(LANES), row(dw), row(dw), row(dvw)],
        out_shape=[bf(gw), bf(gw), bf(gw), bf(gw), jax.ShapeDtypeStruct((n, LANES), F32),
                   bf(dw), bf(dw), bf(dvw)],
        scratch_shapes=[pltpu.VMEM((rows + 8, 3 * gw), F32)],
        compiler_params=pltpu.CompilerParams(
            dimension_semantics=("arbitrary",), vmem_limit_bytes=VMEM_LIMIT_BYTES),
        name="in_proj",
    )(x1, w_cat, conv_w.astype(F32), cos_t, sin_lo, sin_hi)


def _segment_cumsum(x, seg, reverse=False):
    width = x.shape[1]
    pos = lax.broadcasted_iota(jnp.int32, x.shape, 1) % seg
    step = 1
    while step < seg:
        if reverse:
            shifted = pltpu.roll(x, width - step, axis=1)
            keep = pos < seg - step
        else:
            shifted = pltpu.roll(x, step, axis=1)
            keep = pos >= step
        x = x + jnp.where(keep, shifted, 0.0)
        step *= 2
    return x


def _unit_lower_inverse(lows, ci, cj):
    c = lows[0].shape[0]
    base = min(INV_BASE, c)
    shift = base.bit_length() - 1
    same_block = (ci >> shift) == (cj >> shift)
    eye = (ci == cj).astype(F32)
    diags = [jnp.where(same_block, low, 0.0) for low in lows]
    invs = [eye - d for d in diags]
    dbs = [d.astype(BF16) for d in diags]
    powers = [_dot(d, d) for d in dbs]
    span = 2
    while True:
        pbs = [p.astype(BF16) for p in powers]
        invs = [inv + _dot(inv.astype(BF16), pb) for inv, pb in zip(invs, pbs)]
        span *= 2
        if span >= base:
            break
        powers = [_dot(pb, pb) for pb in pbs]
    size = base
    while size < c:
        shift = size.bit_length() - 1
        sibling = ((ci >> shift) ^ (cj >> shift)) == 1
        offs = [jnp.where(sibling, low, 0.0).astype(BF16) for low in lows]
        ibs = [inv.astype(BF16) for inv in invs]
        halves = [_dot(ib, off).astype(BF16) for ib, off in zip(ibs, offs)]
        invs = [inv - _dot(half, ib) for inv, half, ib in zip(invs, halves, ibs)]
        size *= 2
    return invs


def _gdn_kernel(q_ref, k_ref, v_ref, z_ref, ab_ref, alog_ref, dtb_ref, normw_ref, y_ref,
                state_ref, *, chunk):
    step_len = q_ref.shape[1]
    nh = GDN_HEADS
    n_chunks = step_len // chunk

    @pl.when(pl.program_id(1) == 0)
    def _():
        state_ref[...] = jnp.zeros(state_ref.shape, F32)

    ab_t = ab_ref[0].T[0:2 * nh, :]
    row = lax.broadcasted_iota(jnp.int32, ab_t.shape, 0)
    sp_in = ab_t + dtb_ref[...]
    softplus = jnp.maximum(sp_in, 0.0) + jnp.log(1.0 + jnp.exp(-jnp.abs(sp_in)))
    g_t = jnp.where(row < nh, -jnp.exp(alog_ref[...]) * softplus, 0.0)
    beta_t = pltpu.roll(_sigmoid(ab_t), nh, axis=0)
    gc_t = _segment_cumsum(g_t, chunk)
    g_rev = _segment_cumsum(g_t, chunk, reverse=True)
    tail_t = g_rev - g_t
    egc_t = jnp.exp(gc_t)
    rows_t = jnp.concatenate(
        [gc_t, egc_t, jnp.exp(tail_t), beta_t, jnp.zeros((LANES - 4 * 2 * nh, step_len), F32)], axis=0)
    cols = rows_t.T
    bg_t = beta_t * egc_t
    eglast_t = jnp.exp(gc_t + tail_t)

    ci = lax.broadcasted_iota(jnp.int32, (chunk, chunk), 0)
    cj = lax.broadcasted_iota(jnp.int32, (chunk, chunk), 1)

    probs = [(c, hd) for c in range(n_chunks) for hd in range(nh)]
    tile = lambda ref, c, hd: ref[0, c * chunk:(c + 1) * chunk, hd * GDN_DK:(hd + 1) * GDN_DK]
    col = lambda quantity, c, hd: cols[c * chunk:(c + 1) * chunk, 8 * quantity + hd:8 * quantity + hd + 1]
    rowv = lambda t, c, hd: t[hd:hd + 1, c * chunk:(c + 1) * chunk]
    kb = [tile(k_ref, c, hd) for c, hd in probs]
    qb = [tile(q_ref, c, hd) for c, hd in probs]
    decay = [jnp.where(ci >= cj, jnp.exp(jnp.minimum(col(0, c, hd) - rowv(gc_t, c, hd), 0.0)), 0.0)
             for c, hd in probs]
    kk = [_dot_nt(b, b) for b in kb]
    qk = [(_dot_nt(a, b) * d).astype(BF16) for a, b, d in zip(qb, kb, decay)]
    lows = [jnp.where(ci > cj, m * d * col(3, c, hd), 0.0) for m, d, (c, hd) in zip(kk, decay, probs)]
    t_mats = _unit_lower_inverse(lows, ci, cj)
    us = [_dot((t * rowv(beta_t, c, hd)).astype(BF16), tile(v_ref, c, hd))
          for t, (c, hd) in zip(t_mats, probs)]
    ws = [_dot((t * rowv(bg_t, c, hd)).astype(BF16), b).astype(BF16)
          for t, b, (c, hd) in zip(t_mats, kb, probs)]
    q_dec = [(a.astype(F32) * col(1, c, hd)).astype(BF16) for a, (c, hd) in zip(qb, probs)]
    k_dec = [(b.astype(F32) * col(2, c, hd)).astype(BF16) for b, (c, hd) in zip(kb, probs)]

    states = [state_ref[hd] for hd in range(nh)]
    outs = [[] for _ in range(nh)]
    for c in range(n_chunks):
        idx = [c * nh + hd for hd in range(nh)]
        sbs = [s.astype(BF16) for s in states]
        v_new = [us[i] - _dot(ws[i], sb) for i, sb in zip(idx, sbs)]
        vnb = [v.astype(BF16) for v in v_new]
        states = [s * eglast_t[hd:hd + 1, (c + 1) * chunk - 1:(c + 1) * chunk] + _dot_tn(k_dec[i], vb)
                  for hd, (s, i, vb) in enumerate(zip(states, idx, vnb))]
        for hd, (i, sb, vb) in enumerate(zip(idx, sbs, vnb)):
            outs[hd].append(_dot(q_dec[i], sb) + _dot(qk[i], vb))
    normw = normw_ref[...]
    for hd in range(nh):
        lo = hd * GDN_DV
        state_ref[hd] = states[hd]
        o = jnp.concatenate(outs[hd], axis=0)
        o = o * lax.rsqrt(jnp.mean(o * o, axis=-1, keepdims=True) + RMS_EPS) * normw
        z = z_ref[0, :, lo:lo + GDN_DV].astype(F32)
        y_ref[0, :, lo:lo + GDN_DV] = (o * (z * _sigmoid(z))).astype(y_ref.dtype)


def _gdn_call(q, k, v, z, ab, a_log, dt_bias, norm_w):
    bsz, seq, width = q.shape
    nh = GDN_HEADS
    step_len = min(GDN_STEP, seq)
    chunk = min(GDN_CHUNK, step_len)
    pad = lambda t: jnp.concatenate([t.astype(F32), jnp.zeros((nh,), F32)])[:, None]
    blk = lambda w: pl.BlockSpec((1, step_len, w), lambda b, t: (b, t, 0))
    return pl.pallas_call(
        functools.partial(_gdn_kernel, chunk=chunk),
        grid=(bsz, seq // step_len),
        in_specs=[blk(width), blk(width), blk(width), blk(width), blk(LANES),
                  _const_spec((2 * nh, 1)), _const_spec((2 * nh, 1)), _const_spec((1, GDN_DV))],
        out_specs=blk(width),
        out_shape=jax.ShapeDtypeStruct((bsz, seq, width), BF16),
        scratch_shapes=[pltpu.VMEM((nh, GDN_DK, GDN_DV), F32)],
        compiler_params=pltpu.CompilerParams(
            dimension_semantics=("parallel", "arbitrary"), vmem_limit_bytes=VMEM_LIMIT_BYTES),
        name="gdn_mixer",
    )(q, k, v, z, ab, pad(a_log), pad(dt_bias), norm_w.astype(F32)[None, :])


def _diffattn_kernel(q_ref, k_ref, v_ref, lq1_ref, lk1_ref, lq2_ref, lk2_ref, w_ref, o_ref,
                     qm_ref, acc_ref, *, lam_init):
    blk = q_ref.shape[1]
    nh = q_ref.shape[2] // LANES
    n_maps = 2 * nh
    qi = pl.program_id(1)
    lane = lax.broadcasted_iota(jnp.int32, (blk, LANES), 1)
    for hd in range(nh):
        q = q_ref[0, :, hd * LANES:(hd + 1) * LANES]
        zero = jnp.zeros_like(q)
        qm_ref[2 * hd] = jnp.where(lane < DIFF_DQK, q, zero)
        qm_ref[2 * hd + 1] = jnp.where(lane >= DIFF_DQK, q, zero)
    acc_ref[...] = jnp.zeros(acc_ref.shape, F32)

    def block(j, carry, masked):
        start = pl.multiple_of(j * blk, blk)
        head_cols = lambda ref, c: ref[0, pl.ds(start, blk), (c // 2) * LANES:(c // 2 + 1) * LANES]
        scores = [_dot_nt(head_cols(k_ref, c), qm_ref[c]) for c in range(n_maps)]
        if masked:
            kv_pos = lax.broadcasted_iota(jnp.int32, (blk, blk), 0)
            q_pos = lax.broadcasted_iota(jnp.int32, (blk, blk), 1)
            scores = [jnp.where(q_pos >= kv_pos, s, -jnp.inf) for s in scores]
        stats, scales, probs = [], [], []
        for c, s in enumerate(scores):
            m, l = carry[2 * c], carry[2 * c + 1]
            m_new = jnp.maximum(m, jnp.max(s, axis=0, keepdims=True))
            scale = jnp.exp2(m - m_new)
            p = jnp.exp2(s - m_new)
            stats += [m_new, scale * l + jnp.sum(p, axis=0, keepdims=True)]
            scales.append(scale)
            probs.append(p.astype(BF16))
        for c in range(n_maps):
            acc_ref[c] = acc_ref[c] * scales[c] + _dot_tn(head_cols(v_ref, c), probs[c])
        return tuple(stats)

    init = (jnp.full((1, blk), -jnp.inf, F32), jnp.zeros((1, blk), F32)) * n_maps
    carry = lax.fori_loop(0, qi, functools.partial(block, masked=False), init)
    stats = block(qi, carry, masked=True)

    lam = (jnp.exp(jnp.sum(lq1_ref[...] * lk1_ref[...], keepdims=True))
           - jnp.exp(jnp.sum(lq2_ref[...] * lk2_ref[...], keepdims=True)) + lam_init)
    w_row = w_ref[...] * (1.0 - lam_init)
    for hd in range(nh):
        l1, l2 = stats[4 * hd + 1], stats[4 * hd + 3]
        o = acc_ref[2 * hd] / l1 - lam * (acc_ref[2 * hd + 1] / l2)
        o = o * lax.rsqrt(jnp.mean(o * o, axis=0, keepdims=True) + RMS_EPS)
        o_ref[0, :, hd * LANES:(hd + 1) * LANES] = (o.T * w_row).astype(o_ref.dtype)


def _diffattn_call(dq, dk, dv, lq1, lk1, lq2, lk2, subln_w, lam_init):
    bsz, seq, width = dq.shape
    blk = min(ATT_BLOCK, seq)
    vec = lambda t: t.astype(F32)[None, :]
    q_spec = pl.BlockSpec((1, blk, width), lambda b, i: (b, i, 0))
    kv_spec = pl.BlockSpec((1, seq, width), lambda b, i: (b, 0, 0))
    return pl.pallas_call(
        functools.partial(_diffattn_kernel, lam_init=lam_init),
        grid=(bsz, seq // blk),
        in_specs=[q_spec, kv_spec, kv_spec,
                  _const_spec((1, DIFF_DQK)), _const_spec((1, DIFF_DQK)),
                  _const_spec((1, DIFF_DQK)), _const_spec((1, DIFF_DQK)),
                  _const_spec((1, DIFF_DV))],
        out_specs=q_spec,
        out_shape=jax.ShapeDtypeStruct((bsz, seq, DIFF_HEADS * DIFF_DV), BF16),
        scratch_shapes=[pltpu.VMEM((2 * DIFF_HEADS, blk, LANES), BF16),
                        pltpu.VMEM((2 * DIFF_HEADS, DIFF_DV, blk), F32)],
        compiler_params=pltpu.CompilerParams(
            dimension_semantics=("parallel", "arbitrary"), vmem_limit_bytes=VMEM_LIMIT_BYTES),
        name="diff_attention",
    )(dq, dk, dv, vec(lq1), vec(lk1), vec(lq2), vec(lk2), vec(subln_w))


def kernel(x, p, ffn1_w13, ffn1_w2, ln1_g, ln1_b, w_in, gdn_conv_w, gdn_a_log, gdn_dt_bias,
           gdn_norm_w, diff_lq1, diff_lk1, diff_lq2, diff_lk2, diff_subln_w, w_out, ln2_g, ln2_b,
           ffn2_w13, ffn2_w2, ple_gate_w, ple_proj_w, ln3_g, ln3_b):
    bsz, seq, d = x.shape
    n = bsz * seq
    gw = GDN_HEADS * GDN_DK
    n_ab = 2 * GDN_HEADS
    xs = x.reshape(n, d)
    for i in range(w_in.shape[0]):
        lam_init = 0.8 - 0.6 * math.exp(-0.3 * i)
        vec = lambda t: t[i].astype(F32)[None, :]
        xs = _ffn1_call(xs, ffn1_w13[i].astype(BF16), ffn1_w2[i].astype(BF16), vec(ln1_g), vec(ln1_b))
        w = w_in[i]
        w_cat = jnp.concatenate(
            [w[:, :4 * gw], w[:, 4 * gw:4 * gw + n_ab], jnp.zeros((d, LANES - n_ab), w.dtype),
             w[:, 4 * gw + n_ab:]], axis=1).astype(BF16)
        gq, gk, gv, z, ab, dq, dk, dv = _inproj_call(xs, w_cat, gdn_conv_w[i], seq)
        shape3 = lambda t: t.reshape(bsz, seq, t.shape[1])
        y_a = _gdn_call(shape3(gq), shape3(gk), shape3(gv), shape3(z), shape3(ab), gdn_a_log[i],
                        gdn_dt_bias[i], gdn_norm_w[i])
        y_b = _diffattn_call(shape3(dq), shape3(dk), shape3(dv), diff_lq1[i], diff_lk1[i],
                             diff_lq2[i], diff_lk2[i], diff_subln_w[i], lam_init)
        wo = w_out[i].astype(BF16)
        xs = _ffn2_call(xs, y_a.reshape(n, -1), y_b.reshape(n, -1), p[i].reshape(n, -1),
                        wo[:gw], wo[gw:], vec(ln2_g), vec(ln2_b),
                        ffn2_w13[i].astype(BF16), ffn2_w2[i].astype(BF16),
                        ple_gate_w[i].astype(BF16), ple_proj_w[i].astype(BF16),
                        vec(ln3_g), vec(ln3_b))
    return xs.reshape(bsz, seq, d)
```

```python
import functools
import math

import jax
import jax.numpy as jnp
from jax import lax
from jax.experimental import pallas as pl
from jax.experimental.pallas import tpu as pltpu

F32 = jnp.float32
BF16 = jnp.bfloat16

LN_EPS = 1e-5
RMS_EPS = 1e-6
DEPTH = 1
DEEPNORM_ALPHA = (2.0 * DEPTH) ** 0.25
ROPE_THETA = 500000.0

GDN_HEADS = 4
GDN_DK = 128
GDN_DV = 128
GDN_CONV = 4
DIFF_HEADS = 4
DIFF_DQK = 64
DIFF_DV = 128
ROPE_DIM = DIFF_DQK // 4

VMEM_LIMIT_BYTES = 56 * 1024 * 1024
LANES = 128

FFN_ROWS = 512
FFN_GROUPS = 2
PROJ_ROWS = 512
GDN_STEP = 1024
GDN_CHUNK = 128
INV_BASE = 16
ATT_BLOCK = 512
ATT_HEADS = 4
ATT_ONES_ROWS = 16


def _const_spec(shape):
    nd = len(shape)
    return pl.BlockSpec(shape, lambda *_: (0,) * nd, pipeline_mode=pl.Buffered(1))


def _sigmoid(x):
    return 1.0 / (1.0 + jnp.exp(-x))


def _layer_norm(y, g, b):
    mu = jnp.mean(y, axis=-1, keepdims=True)
    d = y - mu
    var = jnp.mean(d * d, axis=-1, keepdims=True)
    return d * lax.rsqrt(var + LN_EPS) * g + b


def _dot(a, b):
    return jnp.dot(a, b, preferred_element_type=F32)


def _dot_nt(a, b):
    return lax.dot_general(a, b, (((1,), (1,)), ((), ())), preferred_element_type=F32)


def _dot_tn(a, b):
    return lax.dot_general(a, b, (((0,), (0,)), ((), ())), preferred_element_type=F32)


def _row_groups(rows):
    groups = FFN_GROUPS if rows % (8 * FFN_GROUPS) == 0 else 1
    return [slice(g * rows // groups, (g + 1) * rows // groups) for g in range(groups)]


def _swiglu(xb, w13_ref, w2_ref, d_ff):
    h = _dot(xb, w13_ref[...])
    gate = h[:, :d_ff]
    up = h[:, d_ff:]
    act = (gate * _sigmoid(gate) * up).astype(BF16)
    return _dot(act, w2_ref[...])


def _ffn1_kernel(x_ref, w13_ref, w2_ref, g_ref, b_ref, o_ref, *, d_ff):
    for rows in _row_groups(x_ref.shape[0]):
        x = x_ref[rows, :]
        y = DEEPNORM_ALPHA * x + 0.5 * _swiglu(x.astype(BF16), w13_ref, w2_ref, d_ff)
        o_ref[rows, :] = _layer_norm(y, g_ref[...], b_ref[...])


def _ffn2_kernel(x1_ref, ya_ref, yb_ref, p_ref, wa_ref, wb_ref, g2_ref, b2_ref,
                 w13_ref, w2_ref, wg_ref, wp_ref, g_ref, b_ref, o_ref, *, d_ff):
    for rows in _row_groups(x1_ref.shape[0]):
        mix = _dot(ya_ref[rows, :], wa_ref[...]) + _dot(yb_ref[rows, :], wb_ref[...])
        x = _layer_norm(DEEPNORM_ALPHA * x1_ref[rows, :] + mix, g2_ref[...], b2_ref[...])
        xb = x.astype(BF16)
        ple = _sigmoid(_dot(xb, wg_ref[...])) * _dot(p_ref[rows, :].astype(BF16), wp_ref[...])
        y = DEEPNORM_ALPHA * x + 0.5 * _swiglu(xb, w13_ref, w2_ref, d_ff) + ple
        o_ref[rows, :] = _layer_norm(y, g_ref[...], b_ref[...])


def _ffn1_call(x2d, w13, w2, g, b):
    n, d = x2d.shape
    d_ff = w2.shape[0]
    rows = min(FFN_ROWS, n)
    row_spec = pl.BlockSpec((rows, d), lambda i: (i, 0))
    return pl.pallas_call(
        functools.partial(_ffn1_kernel, d_ff=d_ff),
        grid=(n // rows,),
        in_specs=[row_spec, _const_spec(w13.shape), _const_spec(w2.shape),
                  _const_spec(g.shape), _const_spec(b.shape)],
        out_specs=row_spec,
        out_shape=jax.ShapeDtypeStruct((n, d), F32),
        compiler_params=pltpu.CompilerParams(
            dimension_semantics=("parallel",), vmem_limit_bytes=VMEM_LIMIT_BYTES),
        name="ffn1_ln1",
    )(x2d, w13, w2, g, b)


def _ffn2_call(x1, ya, yb, p2d, wa, wb, g2, b2, w13, w2, wg, wp, g, b):
    n, d = x1.shape
    d_ff = w2.shape[0]
    rows = min(FFN_ROWS, n)
    row = lambda w: pl.BlockSpec((rows, w), lambda i: (i, 0))
    consts = (wa, wb, g2, b2, w13, w2, wg, wp, g, b)
    return pl.pallas_call(
        functools.partial(_ffn2_kernel, d_ff=d_ff),
        grid=(n // rows,),
        in_specs=[row(d), row(ya.shape[1]), row(yb.shape[1]), row(p2d.shape[1])]
                 + [_const_spec(c.shape) for c in consts],
        out_specs=row(d),
        out_shape=jax.ShapeDtypeStruct((n, d), F32),
        compiler_params=pltpu.CompilerParams(
            dimension_semantics=("parallel",), vmem_limit_bytes=VMEM_LIMIT_BYTES),
        name="outproj_ln2_ffn2_ple_ln3",
    )(x1, ya, yb, p2d, *consts)


def _rotary(x, cos_t, sin_lo, sin_hi):
    half = ROPE_DIM // 2
    from_hi = pltpu.roll(x, LANES - half, axis=1)
    from_lo = pltpu.roll(x, half, axis=1)
    return x * cos_t + from_hi * sin_lo + from_lo * sin_hi


def _inproj_kernel(x_ref, w_ref, convw_ref, cos_ref, slo_ref, shi_ref,
                   gq_ref, gk_ref, gv_ref, z_ref, ab_ref, dq_ref, dk_ref, dv_ref, xin_ref,
                   *, q_scale, blocks_per_seq):
    rows = x_ref.shape[0]
    gw = GDN_HEADS * GDN_DK
    dw = DIFF_HEADS * 2 * DIFF_DQK
    base = 4 * gw + LANES
    halo = 8
    xb = x_ref[...].astype(BF16)
    proj = lambda lo, hi: _dot(xb, w_ref[:, lo:hi])

    @pl.when(lax.rem(pl.program_id(0), blocks_per_seq) == 0)
    def _():
        xin_ref[0:halo, :] = jnp.zeros((halo, xin_ref.shape[1]), F32)

    xin_ref[halo:halo + rows, :] = proj(0, 3 * gw)
    h_qk = proj(base, base + 2 * dw)
    conv = None
    for j in range(GDN_CONV):
        start = halo - (GDN_CONV - 1) + j
        term = convw_ref[j:j + 1, :] * xin_ref[start:start + rows, :]
        conv = term if conv is None else conv + term
    xin_ref[0:halo, :] = xin_ref[rows:rows + halo, :]
    act = conv * _sigmoid(conv)
    for hd in range(GDN_HEADS):
        lo = hd * GDN_DK
        q = act[:, lo:lo + GDN_DK]
        k = act[:, gw + lo:gw + lo + GDN_DK]
        q = q * (lax.rsqrt(jnp.sum(q * q, axis=-1, keepdims=True) + RMS_EPS) * GDN_DK ** -0.5)
        k = k * lax.rsqrt(jnp.sum(k * k, axis=-1, keepdims=True) + RMS_EPS)
        gq_ref[:, lo:lo + GDN_DK] = q.astype(BF16)
        gk_ref[:, lo:lo + GDN_DK] = k.astype(BF16)
    gv_ref[...] = act[:, 2 * gw:3 * gw].astype(BF16)
    h_rest = proj(3 * gw, base)
    cos_t, sin_lo, sin_hi = cos_ref[...], slo_ref[...], shi_ref[...]
    for hd in range(DIFF_HEADS):
        lo = hd * LANES
        q = _rotary(h_qk[:, lo:lo + LANES], cos_t, sin_lo, sin_hi)
        dq_ref[:, lo:lo + LANES] = (q * q_scale).astype(BF16)
        k = _rotary(h_qk[:, dw + lo:dw + lo + LANES], cos_t, sin_lo, sin_hi)
        dk_ref[:, lo:lo + LANES] = k.astype(BF16)
    z_ref[...] = h_rest[:, :gw].astype(BF16)
    ab_ref[...] = h_rest[:, gw:]
    dv_ref[...] = proj(base + 2 * dw, base + 3 * dw).astype(BF16)


def _rotary_tables(seq):
    half = ROPE_DIM // 2
    inv_freq = ROPE_THETA ** (-jnp.arange(0, ROPE_DIM, 2, dtype=F32) / ROPE_DIM)
    ang = jnp.arange(seq, dtype=F32)[:, None] * inv_freq[None, :]
    cos, sin = jnp.cos(ang), jnp.sin(ang)
    ones = jnp.ones((seq, DIFF_DQK - ROPE_DIM), F32)
    zeros = jnp.zeros((seq, DIFF_DQK - ROPE_DIM), F32)
    zh = jnp.zeros((seq, half), F32)
    cos_map = jnp.concatenate([cos, cos, ones], axis=1)
    lo_map = jnp.concatenate([-sin, zh, zeros], axis=1)
    hi_map = jnp.concatenate([zh, sin, zeros], axis=1)
    tile2 = lambda t: jnp.concatenate([t, t], axis=1)
    return tile2(cos_map), tile2(lo_map), tile2(hi_map)


def _inproj_call(x1, w_cat, conv_w, seq):
    n, d = x1.shape
    rows = min(PROJ_ROWS, seq)
    blocks_per_seq = seq // rows
    cos_t, sin_lo, sin_hi = _rotary_tables(seq)
    gw = GDN_HEADS * GDN_DK
    dw = DIFF_HEADS * 2 * DIFF_DQK
    dvw = DIFF_HEADS * DIFF_DV
    q_scale = DIFF_DQK ** -0.5 * math.log2(math.e)
    row = lambda w: pl.BlockSpec((rows, w), lambda i: (i, 0))
    tab = pl.BlockSpec((rows, LANES), lambda i: (i % blocks_per_seq, 0))
    bf = lambda w: jax.ShapeDtypeStruct((n, w), BF16)
    return pl.pallas_call(
        functools.partial(_inproj_kernel, q_scale=q_scale, blocks_per_seq=blocks_per_seq),
        grid=(n // rows,),
        in_specs=[row(d), _const_spec(w_cat.shape), _const_spec(conv_w.shape), tab, tab, tab],
        out_specs=[row(gw), row(gw), row(gw), row(gw), row(LANES), row(dw), row(dw), row(dvw)],
        out_shape=[bf(gw), bf(gw), bf(gw), bf(gw), jax.ShapeDtypeStruct((n, LANES), F32),
                   bf(dw), bf(dw), bf(dvw)],
        scratch_shapes=[pltpu.VMEM((rows + 8, 3 * gw), F32)],
        compiler_params=pltpu.CompilerParams(
            dimension_semantics=("arbitrary",), vmem_limit_bytes=VMEM_LIMIT_BYTES),
        name="in_proj",
    )(x1, w_cat, conv_w.astype(F32), cos_t, sin_lo, sin_hi)


def _segment_cumsum(x, seg, reverse=False):
    width = x.shape[1]
    pos = lax.broadcasted_iota(jnp.int32, x.shape, 1) % seg
    step = 1
    while step < seg:
        if reverse:
            shifted = pltpu.roll(x, width - step, axis=1)
            keep = pos < seg - step
        else:
            shifted = pltpu.roll(x, step, axis=1)
            keep = pos >= step
        x = x + jnp.where(keep, shifted, 0.0)
        step *= 2
    return x


def _unit_lower_inverse(lows, ci, cj):
    c = lows[0].shape[0]
    base = min(INV_BASE, c)
    shift = base.bit_length() - 1
    same_block = (ci >> shift) == (cj >> shift)
    eye = (ci == cj).astype(F32)
    diags = [jnp.where(same_block, low, 0.0) for low in lows]
    invs = [eye - d for d in diags]
    dbs = [d.astype(BF16) for d in diags]
    powers = [_dot(d, d) for d in dbs]
    span = 2
    while True:
        pbs = [p.astype(BF16) for p in powers]
        invs = [inv + _dot(inv.astype(BF16), pb) for inv, pb in zip(invs, pbs)]
        span *= 2
        if span >= base:
            break
        powers = [_dot(pb, pb) for pb in pbs]
    size = base
    while size < c:
        shift = size.bit_length() - 1
        sibling = ((ci >> shift) ^ (cj >> shift)) == 1
        offs = [jnp.where(sibling, low, 0.0).astype(BF16) for low in lows]
        ibs = [inv.astype(BF16) for inv in invs]
        halves = [_dot(ib, off).astype(BF16) for ib, off in zip(ibs, offs)]
        invs = [inv - _dot(half, ib) for inv, half, ib in zip(invs, halves, ibs)]
        size *= 2
    return invs


def _gdn_kernel(q_ref, k_ref, v_ref, z_ref, ab_ref, alog_ref, dtb_ref, normw_ref, y_ref,
                state_ref, *, chunk):
    step_len = q_ref.shape[1]
    nh = GDN_HEADS
    n_chunks = step_len // chunk

    @pl.when(pl.program_id(1) == 0)
    def _():
        state_ref[...] = jnp.zeros(state_ref.shape, F32)

    ab_t = ab_ref[0].T[0:2 * nh, :]
    row = lax.broadcasted_iota(jnp.int32, ab_t.shape, 0)
    sp_in = ab_t + dtb_ref[...]
    softplus = jnp.maximum(sp_in, 0.0) + jnp.log(1.0 + jnp.exp(-jnp.abs(sp_in)))
    g_t = jnp.where(row < nh, -jnp.exp(alog_ref[...]) * softplus, 0.0)
    beta_t = pltpu.roll(_sigmoid(ab_t), nh, axis=0)
    gc_t = _segment_cumsum(g_t, chunk)
    g_rev = _segment_cumsum(g_t, chunk, reverse=True)
    tail_t = g_rev - g_t
    egc_t = jnp.exp(gc_t)
    rows_t = jnp.concatenate(
        [gc_t, egc_t, jnp.exp(tail_t), beta_t, jnp.zeros((LANES - 4 * 2 * nh, step_len), F32)], axis=0)
    cols = rows_t.T
    bg_t = beta_t * egc_t
    eglast_t = jnp.exp(gc_t + tail_t)

    ci = lax.broadcasted_iota(jnp.int32, (chunk, chunk), 0)
    cj = lax.broadcasted_iota(jnp.int32, (chunk, chunk), 1)

    probs = [(c, hd) for c in range(n_chunks) for hd in range(nh)]
    tile = lambda ref, c, hd: ref[0, c * chunk:(c + 1) * chunk, hd * GDN_DK:(hd + 1) * GDN_DK]
    col = lambda quantity, c, hd: cols[c * chunk:(c + 1) * chunk, 8 * quantity + hd:8 * quantity + hd + 1]
    rowv = lambda t, c, hd: t[hd:hd + 1, c * chunk:(c + 1) * chunk]
    kb = [tile(k_ref, c, hd) for c, hd in probs]
    qb = [tile(q_ref, c, hd) for c, hd in probs]
    decay = [jnp.where(ci >= cj, jnp.exp(jnp.minimum(col(0, c, hd) - rowv(gc_t, c, hd), 0.0)), 0.0)
             for c, hd in probs]
    kk = [_dot_nt(b, b) for b in kb]
    qk = [(_dot_nt(a, b) * d).astype(BF16) for a, b, d in zip(qb, kb, decay)]
    lows = [jnp.where(ci > cj, m * d * col(3, c, hd), 0.0) for m, d, (c, hd) in zip(kk, decay, probs)]
    t_mats = _unit_lower_inverse(lows, ci, cj)
    us = [_dot((t * rowv(beta_t, c, hd)).astype(BF16), tile(v_ref, c, hd))
          for t, (c, hd) in zip(t_mats, probs)]
    ws = [_dot((t * rowv(bg_t, c, hd)).astype(BF16), b).astype(BF16)
          for t, b, (c, hd) in zip(t_mats, kb, probs)]
    q_dec = [(a.astype(F32) * col(1, c, hd)).astype(BF16) for a, (c, hd) in zip(qb, probs)]
    k_dec = [(b.astype(F32) * col(2, c, hd)).astype(BF16) for b, (c, hd) in zip(kb, probs)]

    states = [state_ref[hd] for hd in range(nh)]
    outs = [[] for _ in range(nh)]
    for c in range(n_chunks):
        idx = [c * nh + hd for hd in range(nh)]
        sbs = [s.astype(BF16) for s in states]
        v_new = [us[i] - _dot(ws[i], sb) for i, sb in zip(idx, sbs)]
        vnb = [v.astype(BF16) for v in v_new]
        states = [s * eglast_t[hd:hd + 1, (c + 1) * chunk - 1:(c + 1) * chunk] + _dot_tn(k_dec[i], vb)
                  for hd, (s, i, vb) in enumerate(zip(states, idx, vnb))]
        for hd, (i, sb, vb) in enumerate(zip(idx, sbs, vnb)):
            outs[hd].append(_dot(q_dec[i], sb) + _dot(qk[i], vb))
    normw = normw_ref[...]
    for hd in range(nh):
        lo = hd * GDN_DV
        state_ref[hd] = states[hd]
        o = jnp.concatenate(outs[hd], axis=0)
        o = o * lax.rsqrt(jnp.mean(o * o, axis=-1, keepdims=True) + RMS_EPS) * normw
        z = z_ref[0, :, lo:lo + GDN_DV].astype(F32)
        y_ref[0, :, lo:lo + GDN_DV] = (o * (z * _sigmoid(z))).astype(y_ref.dtype)


def _gdn_call(q, k, v, z, ab, a_log, dt_bias, norm_w):
    bsz, seq, width = q.shape
    nh = GDN_HEADS
    step_len = min(GDN_STEP, seq)
    chunk = min(GDN_CHUNK, step_len)
    pad = lambda t: jnp.concatenate([t.astype(F32), jnp.zeros((nh,), F32)])[:, None]
    blk = lambda w: pl.BlockSpec((1, step_len, w), lambda b, t: (b, t, 0))
    return pl.pallas_call(
        functools.partial(_gdn_kernel, chunk=chunk),
        grid=(bsz, seq // step_len),
        in_specs=[blk(width), blk(width), blk(width), blk(width), blk(LANES),
                  _const_spec((2 * nh, 1)), _const_spec((2 * nh, 1)), _const_spec((1, GDN_DV))],
        out_specs=blk(width),
        out_shape=jax.ShapeDtypeStruct((bsz, seq, width), BF16),
        scratch_shapes=[pltpu.VMEM((nh, GDN_DK, GDN_DV), F32)],
        compiler_params=pltpu.CompilerParams(
            dimension_semantics=("parallel", "arbitrary"), vmem_limit_bytes=VMEM_LIMIT_BYTES),
        name="gdn_mixer",
    )(q, k, v, z, ab, pad(a_log), pad(dt_bias), norm_w.astype(F32)[None, :])


def _diffattn_kernel(q_ref, k_ref, v_ref, lq1_ref, lk1_ref, lq2_ref, lk2_ref, w_ref, o_ref,
                     qm_ref, acc_ref, *, lam_init):
    blk = q_ref.shape[1]
    nh = q_ref.shape[2] // LANES
    n_maps = 2 * nh
    qi = pl.program_id(2)
    lane = lax.broadcasted_iota(jnp.int32, (blk, LANES), 1)
    for hd in range(nh):
        q = q_ref[0, :, hd * LANES:(hd + 1) * LANES]
        zero = jnp.zeros_like(q)
        qm_ref[2 * hd] = jnp.where(lane < DIFF_DQK, q, zero)
        qm_ref[2 * hd + 1] = jnp.where(lane >= DIFF_DQK, q, zero)
    acc_ref[...] = jnp.zeros(acc_ref.shape, F32)

    n_ones = acc_ref.shape[1] - DIFF_DV

    ones_rows = jnp.ones((n_ones, blk), BF16)

    def block(j, carry, masked):
        start = pl.multiple_of(j * blk, blk)
        head_cols = lambda ref, hd: ref[0, pl.ds(start, blk), hd * LANES:(hd + 1) * LANES]
        scores = [_dot_nt(head_cols(k_ref, c // 2), qm_ref[c]) for c in range(n_maps)]
        if masked:
            kv_pos = lax.broadcasted_iota(jnp.int32, (blk, blk), 0)
            q_pos = lax.broadcasted_iota(jnp.int32, (blk, blk), 1)
            scores = [jnp.where(q_pos >= kv_pos, s, -jnp.inf) for s in scores]
        new_max, scales, probs = [], [], []
        for m, s in zip(carry, scores):
            m_new = jnp.maximum(m, jnp.max(s, axis=0, keepdims=True))
            scales.append(jnp.exp2(m - m_new))
            probs.append(jnp.exp2(s - m_new).astype(BF16))
            new_max.append(m_new)
        vts = [jnp.concatenate([head_cols(v_ref, hd).T, ones_rows], axis=0) for hd in range(nh)]
        for c in range(n_maps):
            acc_ref[c] = acc_ref[c] * scales[c] + _dot(vts[c // 2], probs[c])
        return tuple(new_max)

    init = (jnp.full((1, blk), -jnp.inf, F32),) * n_maps
    carry = lax.fori_loop(0, qi, functools.partial(block, masked=False), init)
    block(qi, carry, masked=True)

    lam = (jnp.exp(jnp.sum(lq1_ref[...] * lk1_ref[...], keepdims=True))
           - jnp.exp(jnp.sum(lq2_ref[...] * lk2_ref[...], keepdims=True)) + lam_init)
    w_row = w_ref[...] * (1.0 - lam_init)
    out_of = lambda c: acc_ref[c, 0:DIFF_DV, :] / acc_ref[c, DIFF_DV:DIFF_DV + 1, :]
    for hd in range(nh):
        o = out_of(2 * hd) - lam * out_of(2 * hd + 1)
        o = o * lax.rsqrt(jnp.mean(o * o, axis=0, keepdims=True) + RMS_EPS)
        o_ref[0, :, hd * LANES:(hd + 1) * LANES] = (o.T * w_row).astype(o_ref.dtype)


def _diffattn_call(dq, dk, dv, lq1, lk1, lq2, lk2, subln_w, lam_init):
    bsz, seq, width = dq.shape
    blk = min(ATT_BLOCK, seq)
    vec = lambda t: t.astype(F32)[None, :]
    hw = ATT_HEADS * LANES
    q_spec = pl.BlockSpec((1, blk, hw), lambda b, g, i: (b, i, g))
    kv_spec = pl.BlockSpec((1, seq, hw), lambda b, g, i: (b, 0, g))
    return pl.pallas_call(
        functools.partial(_diffattn_kernel, lam_init=lam_init),
        grid=(bsz, width // hw, seq // blk),
        in_specs=[q_spec, kv_spec, kv_spec,
                  _const_spec((1, DIFF_DQK)), _const_spec((1, DIFF_DQK)),
                  _const_spec((1, DIFF_DQK)), _const_spec((1, DIFF_DQK)),
                  _const_spec((1, DIFF_DV))],
        out_specs=q_spec,
        out_shape=jax.ShapeDtypeStruct((bsz, seq, DIFF_HEADS * DIFF_DV), BF16),
        scratch_shapes=[pltpu.VMEM((2 * ATT_HEADS, blk, LANES), BF16),
                        pltpu.VMEM((2 * ATT_HEADS, DIFF_DV + ATT_ONES_ROWS, blk), F32)],
        compiler_params=pltpu.CompilerParams(
            dimension_semantics=("parallel", "parallel", "arbitrary"),
            vmem_limit_bytes=VMEM_LIMIT_BYTES),
        name="diff_attention",
    )(dq, dk, dv, vec(lq1), vec(lk1), vec(lq2), vec(lk2), vec(subln_w))


def kernel(x, p, ffn1_w13, ffn1_w2, ln1_g, ln1_b, w_in, gdn_conv_w, gdn_a_log, gdn_dt_bias,
           gdn_norm_w, diff_lq1, diff_lk1, diff_lq2, diff_lk2, diff_subln_w, w_out, ln2_g, ln2_b,
           ffn2_w13, ffn2_w2, ple_gate_w, ple_proj_w, ln3_g, ln3_b):
    bsz, seq, d = x.shape
    n = bsz * seq
    gw = GDN_HEADS * GDN_DK
    n_ab = 2 * GDN_HEADS
    xs = x.reshape(n, d)
    for i in range(w_in.shape[0]):
        lam_init = 0.8 - 0.6 * math.exp(-0.3 * i)
        vec = lambda t: t[i].astype(F32)[None, :]
        xs = _ffn1_call(xs, ffn1_w13[i].astype(BF16), ffn1_w2[i].astype(BF16), vec(ln1_g), vec(ln1_b))
        w = w_in[i]
        w_cat = jnp.concatenate(
            [w[:, :4 * gw], w[:, 4 * gw:4 * gw + n_ab], jnp.zeros((d, LANES - n_ab), w.dtype),
             w[:, 4 * gw + n_ab:]], axis=1).astype(BF16)
        gq, gk, gv, z, ab, dq, dk, dv = _inproj_call(xs, w_cat, gdn_conv_w[i], seq)
        shape3 = lambda t: t.reshape(bsz, seq, t.shape[1])
        y_a = _gdn_call(shape3(gq), shape3(gk), shape3(gv), shape3(z), shape3(ab), gdn_a_log[i],
                        gdn_dt_bias[i], gdn_norm_w[i])
        y_b = _diffattn_call(shape3(dq), shape3(dk), shape3(dv), diff_lq1[i], diff_lk1[i],
                             diff_lq2[i], diff_lk2[i], diff_subln_w[i], lam_init)
        wo = w_out[i].astype(BF16)
        xs = _ffn2_call(xs, y_a.reshape(n, -1), y_b.reshape(n, -1), p[i].reshape(n, -1),
                        wo[:gw], wo[gw:], vec(ln2_g), vec(ln2_b),
                        ffn2_w13[i].astype(BF16), ffn2_w2[i].astype(BF16),
                        ple_gate_w[i].astype(BF16), ple_proj_w[i].astype(BF16),
                        vec(ln3_g), vec(ln3_b))
    return xs.reshape(bsz, seq, d)
```

```python
import functools
import math

import jax
import jax.numpy as jnp
from jax import lax
from jax.experimental import pallas as pl
from jax.experimental.pallas import tpu as pltpu

F32 = jnp.float32
BF16 = jnp.bfloat16

LN_EPS = 1e-5
RMS_EPS = 1e-6
DEPTH = 1
DEEPNORM_ALPHA = (2.0 * DEPTH) ** 0.25
ROPE_THETA = 500000.0

GDN_HEADS = 4
GDN_DK = 128
GDN_DV = 128
GDN_CONV = 4
DIFF_HEADS = 4
DIFF_DQK = 64
DIFF_DV = 128
ROPE_DIM = DIFF_DQK // 4

VMEM_LIMIT_BYTES = 56 * 1024 * 1024
LANES = 128

FFN_ROWS = 512
FFN_GROUPS = 2
PROJ_ROWS = 512
GDN_STEP = 2048
GDN_CHUNK = 128
GDN_GROUPS = 4
INV_BASE = 16
ATT_BLOCK = 512
ATT_HEADS = 4
ATT_ONES_ROWS = 16


def _const_spec(shape):
    nd = len(shape)
    return pl.BlockSpec(shape, lambda *_: (0,) * nd, pipeline_mode=pl.Buffered(1))


def _sigmoid(x):
    return 1.0 / (1.0 + jnp.exp(-x))


def _layer_norm(y, g, b):
    mu = jnp.mean(y, axis=-1, keepdims=True)
    d = y - mu
    var = jnp.mean(d * d, axis=-1, keepdims=True)
    return d * lax.rsqrt(var + LN_EPS) * g + b


def _dot(a, b):
    return jnp.dot(a, b, preferred_element_type=F32)


def _dot_nt(a, b):
    return lax.dot_general(a, b, (((1,), (1,)), ((), ())), preferred_element_type=F32)


def _dot_tn(a, b):
    return lax.dot_general(a, b, (((0,), (0,)), ((), ())), preferred_element_type=F32)


def _row_groups(rows):
    groups = FFN_GROUPS if rows % (8 * FFN_GROUPS) == 0 else 1
    return [slice(g * rows // groups, (g + 1) * rows // groups) for g in range(groups)]


def _swiglu(xb, w13_ref, w2_ref, d_ff):
    h = _dot(xb, w13_ref[...])
    gate = h[:, :d_ff]
    up = h[:, d_ff:]
    act = (gate * _sigmoid(gate) * up).astype(BF16)
    return _dot(act, w2_ref[...])


def _ffn1_kernel(x_ref, w13_ref, w2_ref, g_ref, b_ref, o_ref, *, d_ff):
    for rows in _row_groups(x_ref.shape[0]):
        x = x_ref[rows, :]
        y = DEEPNORM_ALPHA * x + 0.5 * _swiglu(x.astype(BF16), w13_ref, w2_ref, d_ff)
        o_ref[rows, :] = _layer_norm(y, g_ref[...], b_ref[...])


def _ffn2_kernel(x1_ref, ya_ref, yb_ref, p_ref, wa_ref, wb_ref, g2_ref, b2_ref,
                 w13_ref, w2_ref, wg_ref, wp_ref, g_ref, b_ref, o_ref, *, d_ff):
    groups = _row_groups(x1_ref.shape[0])
    xs = []
    for rows in groups:
        mix = _dot(ya_ref[rows, :], wa_ref[...]) + _dot(yb_ref[rows, :], wb_ref[...])
        xs.append(_layer_norm(DEEPNORM_ALPHA * x1_ref[rows, :] + mix, g2_ref[...], b2_ref[...]))
    xbs = [x.astype(BF16) for x in xs]
    ples = [_sigmoid(_dot(xb, wg_ref[...])) * _dot(p_ref[rows, :].astype(BF16), wp_ref[...])
            for rows, xb in zip(groups, xbs)]
    for rows, x, xb, ple in zip(groups, xs, xbs, ples):
        y = DEEPNORM_ALPHA * x + 0.5 * _swiglu(xb, w13_ref, w2_ref, d_ff) + ple
        o_ref[rows, :] = _layer_norm(y, g_ref[...], b_ref[...])


def _ffn1_call(x2d, w13, w2, g, b):
    n, d = x2d.shape
    d_ff = w2.shape[0]
    rows = min(FFN_ROWS, n)
    row_spec = pl.BlockSpec((rows, d), lambda i: (i, 0))
    return pl.pallas_call(
        functools.partial(_ffn1_kernel, d_ff=d_ff),
        grid=(n // rows,),
        in_specs=[row_spec, _const_spec(w13.shape), _const_spec(w2.shape),
                  _const_spec(g.shape), _const_spec(b.shape)],
        out_specs=row_spec,
        out_shape=jax.ShapeDtypeStruct((n, d), F32),
        compiler_params=pltpu.CompilerParams(
            dimension_semantics=("parallel",), vmem_limit_bytes=VMEM_LIMIT_BYTES),
        name="ffn1_ln1",
    )(x2d, w13, w2, g, b)


def _ffn2_call(x1, ya, yb, p2d, wa, wb, g2, b2, w13, w2, wg, wp, g, b):
    n, d = x1.shape
    d_ff = w2.shape[0]
    rows = min(FFN_ROWS, n)
    row = lambda w: pl.BlockSpec((rows, w), lambda i: (i, 0))
    consts = (wa, wb, g2, b2, w13, w2, wg, wp, g, b)
    return pl.pallas_call(
        functools.partial(_ffn2_kernel, d_ff=d_ff),
        grid=(n // rows,),
        in_specs=[row(d), row(ya.shape[1]), row(yb.shape[1]), row(p2d.shape[1])]
                 + [_const_spec(c.shape) for c in consts],
        out_specs=row(d),
        out_shape=jax.ShapeDtypeStruct((n, d), F32),
        compiler_params=pltpu.CompilerParams(
            dimension_semantics=("parallel",), vmem_limit_bytes=VMEM_LIMIT_BYTES),
        name="outproj_ln2_ffn2_ple_ln3",
    )(x1, ya, yb, p2d, *consts)


def _rotary(x, cos_t, sin_lo, sin_hi):
    half = ROPE_DIM // 2
    from_hi = pltpu.roll(x, LANES - half, axis=1)
    from_lo = pltpu.roll(x, half, axis=1)
    return x * cos_t + from_hi * sin_lo + from_lo * sin_hi


def _inproj_kernel(x_ref, w_ref, convw_ref, cos_ref, slo_ref, shi_ref,
                   gq_ref, gk_ref, gv_ref, z_ref, ab_ref, dq_ref, dk_ref, dv_ref, xin_ref,
                   *, q_scale, blocks_per_seq):
    rows = x_ref.shape[0]
    gw = GDN_HEADS * GDN_DK
    dw = DIFF_HEADS * 2 * DIFF_DQK
    base = 4 * gw + LANES
    halo = 8
    xb = x_ref[...].astype(BF16)
    proj = lambda lo, hi: _dot(xb, w_ref[:, lo:hi])

    @pl.when(lax.rem(pl.program_id(0), blocks_per_seq) == 0)
    def _():
        xin_ref[0:halo, :] = jnp.zeros((halo, xin_ref.shape[1]), F32)

    xin_ref[halo:halo + rows, :] = proj(0, 3 * gw)
    h_qk = proj(base, base + 2 * dw)
    conv = None
    for j in range(GDN_CONV):
        start = halo - (GDN_CONV - 1) + j
        term = convw_ref[j:j + 1, :] * xin_ref[start:start + rows, :]
        conv = term if conv is None else conv + term
    xin_ref[0:halo, :] = xin_ref[rows:rows + halo, :]
    act = conv * _sigmoid(conv)
    for hd in range(GDN_HEADS):
        lo = hd * GDN_DK
        q = act[:, lo:lo + GDN_DK]
        k = act[:, gw + lo:gw + lo + GDN_DK]
        q = q * (lax.rsqrt(jnp.sum(q * q, axis=-1, keepdims=True) + RMS_EPS) * GDN_DK ** -0.5)
        k = k * lax.rsqrt(jnp.sum(k * k, axis=-1, keepdims=True) + RMS_EPS)
        gq_ref[:, lo:lo + GDN_DK] = q.astype(BF16)
        gk_ref[:, lo:lo + GDN_DK] = k.astype(BF16)
    gv_ref[...] = act[:, 2 * gw:3 * gw].astype(BF16)
    h_rest = proj(3 * gw, base)
    cos_t, sin_lo, sin_hi = cos_ref[...], slo_ref[...], shi_ref[...]
    for hd in range(DIFF_HEADS):
        lo = hd * LANES
        q = _rotary(h_qk[:, lo:lo + LANES], cos_t, sin_lo, sin_hi)
        dq_ref[:, lo:lo + LANES] = (q * q_scale).astype(BF16)
        k = _rotary(h_qk[:, dw + lo:dw + lo + LANES], cos_t, sin_lo, sin_hi)
        dk_ref[:, lo:lo + LANES] = k.astype(BF16)
    z_ref[...] = h_rest[:, :gw].astype(BF16)
    ab_ref[...] = h_rest[:, gw:]
    dv_ref[...] = proj(base + 2 * dw, base + 3 * dw).astype(BF16)


def _rotary_tables(seq):
    half = ROPE_DIM // 2
    in_map = jnp.arange(LANES, dtype=jnp.int32) % DIFF_DQK
    pair = (in_map % half).astype(F32)
    inv_freq = ROPE_THETA ** (-(2.0 * pair) / ROPE_DIM)
    ang = jnp.arange(seq, dtype=F32)[:, None] * inv_freq[None, :]
    cos, sin = jnp.cos(ang), jnp.sin(ang)
    first, second = in_map < half, (in_map >= half) & (in_map < ROPE_DIM)
    cos_t = jnp.where(first | second, cos, 1.0)
    sin_lo = jnp.where(first, -sin, 0.0)
    sin_hi = jnp.where(second, sin, 0.0)
    return cos_t, sin_lo, sin_hi


def _inproj_call(x1, w_cat, conv_w, seq):
    n, d = x1.shape
    rows = min(PROJ_ROWS, seq)
    blocks_per_seq = seq // rows
    cos_t, sin_lo, sin_hi = _rotary_tables(seq)
    gw = GDN_HEADS * GDN_DK
    dw = DIFF_HEADS * 2 * DIFF_DQK
    dvw = DIFF_HEADS * DIFF_DV
    q_scale = DIFF_DQK ** -0.5 * math.log2(math.e)
    row = lambda w: pl.BlockSpec((rows, w), lambda i: (i, 0))
    tab = pl.BlockSpec((rows, LANES), lambda i: (i % blocks_per_seq, 0))
    bf = lambda w: jax.ShapeDtypeStruct((n, w), BF16)
    return pl.pallas_call(
        functools.partial(_inproj_kernel, q_scale=q_scale, blocks_per_seq=blocks_per_seq),
        grid=(n // rows,),
        in_specs=[row(d), _const_spec(w_cat.shape), _const_spec(conv_w.shape), tab, tab, tab],
        out_specs=[row(gw), row(gw), row(gw), row(gw), row(LANES), row(dw), row(dw), row(dvw)],
        out_shape=[bf(gw), bf(gw), bf(gw), bf(gw), jax.ShapeDtypeStruct((n, LANES), F32),
                   bf(dw), bf(dw), bf(dvw)],
        scratch_shapes=[pltpu.VMEM((rows + 8, 3 * gw), F32)],
        compiler_params=pltpu.CompilerParams(
            dimension_semantics=("arbitrary",), vmem_limit_bytes=VMEM_LIMIT_BYTES),
        name="in_proj",
    )(x1, w_cat, conv_w.astype(F32), cos_t, sin_lo, sin_hi)


def _segment_cumsum(x, seg, reverse=False):
    width = x.shape[1]
    pos = lax.broadcasted_iota(jnp.int32, x.shape, 1) % seg
    step = 1
    while step < seg:
        if reverse:
            shifted = pltpu.roll(x, width - step, axis=1)
            keep = pos < seg - step
        else:
            shifted = pltpu.roll(x, step, axis=1)
            keep = pos >= step
        x = x + jnp.where(keep, shifted, 0.0)
        step *= 2
    return x


def _unit_lower_inverse(lows, ci, cj):
    c = lows[0].shape[0]
    base = min(INV_BASE, c)
    shift = base.bit_length() - 1
    same_block = (ci >> shift) == (cj >> shift)
    eye = (ci == cj).astype(F32)
    diags = [jnp.where(same_block, low, 0.0) for low in lows]
    invs = [eye - d for d in diags]
    dbs = [d.astype(BF16) for d in diags]
    powers = [_dot(d, d) for d in dbs]
    yield
    span = 2
    while True:
        pbs = [p.astype(BF16) for p in powers]
        invs = [inv + _dot(inv.astype(BF16), pb) for inv, pb in zip(invs, pbs)]
        span *= 2
        if span >= base:
            break
        powers = [_dot(pb, pb) for pb in pbs]
        yield
    size = base
    while size < c:
        yield
        shift = size.bit_length() - 1
        sibling = ((ci >> shift) ^ (cj >> shift)) == 1
        offs = [jnp.where(sibling, low, 0.0).astype(BF16) for low in lows]
        ibs = [inv.astype(BF16) for inv in invs]
        halves = [_dot(ib, off).astype(BF16) for ib, off in zip(ibs, offs)]
        yield
        invs = [inv - _dot(half, ib) for inv, half, ib in zip(invs, halves, ibs)]
        size *= 2
    return invs


def _prepare_stage_count(c):
    base = min(INV_BASE, c)
    log2 = lambda v: v.bit_length() - 1
    inverse = 1 + max(log2(base) - 2, 0) + 2 * (log2(c) - log2(base))
    return 1 + inverse + 2


def _gdn_kernel(q_ref, k_ref, v_ref, z_ref, ab_ref, alog_ref, dtb_ref, normw_ref, y_ref,
                state_ref, *, chunk):
    step_len = q_ref.shape[1]
    nh = GDN_HEADS
    n_chunks = step_len // chunk

    @pl.when(pl.program_id(1) == 0)
    def _():
        state_ref[...] = jnp.zeros(state_ref.shape, F32)

    ab_t = ab_ref[0].T[0:2 * nh, :]
    row = lax.broadcasted_iota(jnp.int32, ab_t.shape, 0)
    sp_in = ab_t + dtb_ref[...]
    softplus = jnp.maximum(sp_in, 0.0) + jnp.log(1.0 + jnp.exp(-jnp.abs(sp_in)))
    g_t = jnp.where(row < nh, -jnp.exp(alog_ref[...]) * softplus, 0.0)
    beta_t = pltpu.roll(_sigmoid(ab_t), nh, axis=0)
    gc_t = _segment_cumsum(g_t, chunk)
    g_rev = _segment_cumsum(g_t, chunk, reverse=True)
    tail_t = g_rev - g_t
    egc_t = jnp.exp(gc_t)
    rows_t = jnp.concatenate(
        [gc_t, egc_t, jnp.exp(tail_t), beta_t, jnp.zeros((LANES - 4 * 2 * nh, step_len), F32)], axis=0)
    cols = rows_t.T
    bg_t = beta_t * egc_t
    eglast_t = jnp.exp(gc_t + tail_t)

    ci = lax.broadcasted_iota(jnp.int32, (chunk, chunk), 0)
    cj = lax.broadcasted_iota(jnp.int32, (chunk, chunk), 1)

    tile = lambda ref, c, hd: ref[0, c * chunk:(c + 1) * chunk, hd * GDN_DK:(hd + 1) * GDN_DK]
    col = lambda quantity, c, hd: cols[c * chunk:(c + 1) * chunk, 8 * quantity + hd:8 * quantity + hd + 1]
    rowv = lambda t, c, hd: t[hd:hd + 1, c * chunk:(c + 1) * chunk]
    ready = {}

    def prepare(chunks):
        probs = [(c, hd) for c in chunks for hd in range(nh)]
        kb = [tile(k_ref, c, hd) for c, hd in probs]
        qb = [tile(q_ref, c, hd) for c, hd in probs]
        decay = [jnp.where(ci >= cj, jnp.exp(jnp.minimum(col(0, c, hd) - rowv(gc_t, c, hd), 0.0)), 0.0)
                 for c, hd in probs]
        kk = [_dot_nt(b, b) for b in kb]
        qk = [(_dot_nt(a, b) * d).astype(BF16) for a, b, d in zip(qb, kb, decay)]
        yield
        lows = [jnp.where(ci > cj, m * d * col(3, c, hd), 0.0) for m, d, (c, hd) in zip(kk, decay, probs)]
        t_mats = yield from _unit_lower_inverse(lows, ci, cj)
        yield
        us = [_dot((t * rowv(beta_t, c, hd)).astype(BF16), tile(v_ref, c, hd))
              for t, (c, hd) in zip(t_mats, probs)]
        ws = [_dot((t * rowv(bg_t, c, hd)).astype(BF16), b).astype(BF16)
              for t, b, (c, hd) in zip(t_mats, kb, probs)]
        yield
        for i, (c, hd) in enumerate(probs):
            q_dec = (qb[i].astype(F32) * col(1, c, hd)).astype(BF16)
            k_dec = (kb[i].astype(F32) * col(2, c, hd)).astype(BF16)
            ready[c, hd] = (us[i], ws[i], q_dec, k_dec, qk[i])

    states = [state_ref[hd] for hd in range(nh)]
    outs = [[] for _ in range(nh)]

    def recur(chunks):
        for c in chunks:
            ops = [ready.pop((c, hd)) for hd in range(nh)]
            sbs = [s.astype(BF16) for s in states]
            vnb = [(u - _dot(w, sb)).astype(BF16) for (u, w, _, _, _), sb in zip(ops, sbs)]
            for hd in range(nh):
                eg = eglast_t[hd:hd + 1, (c + 1) * chunk - 1:(c + 1) * chunk]
                states[hd] = states[hd] * eg + _dot_tn(ops[hd][3], vnb[hd])
                outs[hd].append(_dot(ops[hd][2], sbs[hd]) + _dot(ops[hd][4], vnb[hd]))
            yield

    per_group = max(n_chunks // GDN_GROUPS, 1)
    groups = [range(g, min(g + per_group, n_chunks)) for g in range(0, n_chunks, per_group)]
    for _ in prepare(groups[0]):
        pass
    for prev, nxt in zip(groups, groups[1:] + [None]):
        chain = recur(prev)
        if nxt is not None:
            every = max(_prepare_stage_count(chunk) // len(prev), 1)
            for n, _ in enumerate(prepare(nxt)):
                if n % every == every - 1:
                    next(chain, None)
        for _ in chain:
            pass
    normw = normw_ref[...]
    for hd in range(nh):
        lo = hd * GDN_DV
        state_ref[hd] = states[hd]
        o = jnp.concatenate(outs[hd], axis=0)
        o = o * lax.rsqrt(jnp.mean(o * o, axis=-1, keepdims=True) + RMS_EPS) * normw
        z = z_ref[0, :, lo:lo + GDN_DV].astype(F32)
        y_ref[0, :, lo:lo + GDN_DV] = (o * (z * _sigmoid(z))).astype(y_ref.dtype)


def _gdn_call(q, k, v, z, ab, a_log, dt_bias, norm_w):
    bsz, seq, width = q.shape
    nh = GDN_HEADS
    step_len = min(GDN_STEP, seq)
    chunk = min(GDN_CHUNK, step_len)
    pad = lambda t: jnp.concatenate([t.astype(F32), jnp.zeros((nh,), F32)])[:, None]
    blk = lambda w: pl.BlockSpec((1, step_len, w), lambda b, t: (b, t, 0))
    return pl.pallas_call(
        functools.partial(_gdn_kernel, chunk=chunk),
        grid=(bsz, seq // step_len),
        in_specs=[blk(width), blk(width), blk(width), blk(width), blk(LANES),
                  _const_spec((2 * nh, 1)), _const_spec((2 * nh, 1)), _const_spec((1, GDN_DV))],
        out_specs=blk(width),
        out_shape=jax.ShapeDtypeStruct((bsz, seq, width), BF16),
        scratch_shapes=[pltpu.VMEM((nh, GDN_DK, GDN_DV), F32)],
        compiler_params=pltpu.CompilerParams(
            dimension_semantics=("parallel", "arbitrary"), vmem_limit_bytes=VMEM_LIMIT_BYTES),
        name="gdn_mixer",
    )(q, k, v, z, ab, pad(a_log), pad(dt_bias), norm_w.astype(F32)[None, :])


def _diffattn_kernel(q_ref, k_ref, v_ref, lq1_ref, lk1_ref, lq2_ref, lk2_ref, w_ref, o_ref,
                     qm_ref, acc_ref, *, lam_init):
    blk = q_ref.shape[1]
    nh = q_ref.shape[2] // LANES
    n_maps = 2 * nh
    qi = pl.program_id(2)
    lane = lax.broadcasted_iota(jnp.int32, (blk, LANES), 1)
    for hd in range(nh):
        q = q_ref[0, :, hd * LANES:(hd + 1) * LANES]
        zero = jnp.zeros_like(q)
        qm_ref[2 * hd] = jnp.where(lane < DIFF_DQK, q, zero)
        qm_ref[2 * hd + 1] = jnp.where(lane >= DIFF_DQK, q, zero)
    acc_ref[...] = jnp.zeros(acc_ref.shape, F32)

    n_ones = acc_ref.shape[1] - DIFF_DV

    ones_rows = jnp.ones((n_ones, blk), BF16)

    def block(j, carry, masked):
        start = pl.multiple_of(j * blk, blk)
        head_cols = lambda ref, hd: ref[0, pl.ds(start, blk), hd * LANES:(hd + 1) * LANES]
        scores = [_dot_nt(head_cols(k_ref, c // 2), qm_ref[c]) for c in range(n_maps)]
        if masked:
            kv_pos = lax.broadcasted_iota(jnp.int32, (blk, blk), 0)
            q_pos = lax.broadcasted_iota(jnp.int32, (blk, blk), 1)
            scores = [jnp.where(q_pos >= kv_pos, s, -jnp.inf) for s in scores]
        new_max, scales, probs = [], [], []
        for m, s in zip(carry, scores):
            m_new = jnp.maximum(m, jnp.max(s, axis=0, keepdims=True))
            scales.append(jnp.exp2(m - m_new))
            probs.append(jnp.exp2(s - m_new).astype(BF16))
            new_max.append(m_new)
        vts = [jnp.concatenate([head_cols(v_ref, hd).T, ones_rows], axis=0) for hd in range(nh)]
        for c in range(n_maps):
            acc_ref[c] = acc_ref[c] * scales[c] + _dot(vts[c // 2], probs[c])
        return tuple(new_max)

    init = (jnp.full((1, blk), -jnp.inf, F32),) * n_maps
    carry = lax.fori_loop(0, qi, functools.partial(block, masked=False), init)
    block(qi, carry, masked=True)

    lam = (jnp.exp(jnp.sum(lq1_ref[...] * lk1_ref[...], keepdims=True))
           - jnp.exp(jnp.sum(lq2_ref[...] * lk2_ref[...], keepdims=True)) + lam_init)
    w_row = w_ref[...] * (1.0 - lam_init)
    out_of = lambda c: acc_ref[c, 0:DIFF_DV, :] / acc_ref[c, DIFF_DV:DIFF_DV + 1, :]
    for hd in range(nh):
        o = out_of(2 * hd) - lam * out_of(2 * hd + 1)
        o = o * lax.rsqrt(jnp.mean(o * o, axis=0, keepdims=True) + RMS_EPS)
        o_ref[0, :, hd * LANES:(hd + 1) * LANES] = (o.T * w_row).astype(o_ref.dtype)


def _diffattn_call(dq, dk, dv, lq1, lk1, lq2, lk2, subln_w, lam_init):
    bsz, seq, width = dq.shape
    blk = min(ATT_BLOCK, seq)
    vec = lambda t: t.astype(F32)[None, :]
    hw = ATT_HEADS * LANES
    q_spec = pl.BlockSpec((1, blk, hw), lambda b, g, i: (b, i, g))
    kv_spec = pl.BlockSpec((1, seq, hw), lambda b, g, i: (b, 0, g))
    return pl.pallas_call(
        functools.partial(_diffattn_kernel, lam_init=lam_init),
        grid=(bsz, width // hw, seq // blk),
        in_specs=[q_spec, kv_spec, kv_spec,
                  _const_spec((1, DIFF_DQK)), _const_spec((1, DIFF_DQK)),
                  _const_spec((1, DIFF_DQK)), _const_spec((1, DIFF_DQK)),
                  _const_spec((1, DIFF_DV))],
        out_specs=q_spec,
        out_shape=jax.ShapeDtypeStruct((bsz, seq, DIFF_HEADS * DIFF_DV), BF16),
        scratch_shapes=[pltpu.VMEM((2 * ATT_HEADS, blk, LANES), BF16),
                        pltpu.VMEM((2 * ATT_HEADS, DIFF_DV + ATT_ONES_ROWS, blk), F32)],
        compiler_params=pltpu.CompilerParams(
            dimension_semantics=("parallel", "parallel", "arbitrary"),
            vmem_limit_bytes=VMEM_LIMIT_BYTES),
        name="diff_attention",
    )(dq, dk, dv, vec(lq1), vec(lk1), vec(lq2), vec(lk2), vec(subln_w))


def kernel(x, p, ffn1_w13, ffn1_w2, ln1_g, ln1_b, w_in, gdn_conv_w, gdn_a_log, gdn_dt_bias,
           gdn_norm_w, diff_lq1, diff_lk1, diff_lq2, diff_lk2, diff_subln_w, w_out, ln2_g, ln2_b,
           ffn2_w13, ffn2_w2, ple_gate_w, ple_proj_w, ln3_g, ln3_b):
    bsz, seq, d = x.shape
    n = bsz * seq
    gw = GDN_HEADS * GDN_DK
    n_ab = 2 * GDN_HEADS
    xs = x.reshape(n, d)
    for i in range(w_in.shape[0]):
        lam_init = 0.8 - 0.6 * math.exp(-0.3 * i)
        vec = lambda t: t[i].astype(F32)[None, :]
        xs = _ffn1_call(xs, ffn1_w13[i].astype(BF16), ffn1_w2[i].astype(BF16), vec(ln1_g), vec(ln1_b))
        w = w_in[i]
        w_cat = jnp.concatenate(
            [w[:, :4 * gw], w[:, 4 * gw:4 * gw + n_ab], jnp.zeros((d, LANES - n_ab), w.dtype),
             w[:, 4 * gw + n_ab:]], axis=1).astype(BF16)
        gq, gk, gv, z, ab, dq, dk, dv = _inproj_call(xs, w_cat, gdn_conv_w[i], seq)
        shape3 = lambda t: t.reshape(bsz, seq, t.shape[1])
        y_a = _gdn_call(shape3(gq), shape3(gk), shape3(gv), shape3(z), shape3(ab), gdn_a_log[i],
                        gdn_dt_bias[i], gdn_norm_w[i])
        y_b = _diffattn_call(shape3(dq), shape3(dk), shape3(dv), diff_lq1[i], diff_lk1[i],
                             diff_lq2[i], diff_lk2[i], diff_subln_w[i], lam_init)
        wo = w_out[i].astype(BF16)
        xs = _ffn2_call(xs, y_a.reshape(n, -1), y_b.reshape(n, -1), p[i].reshape(n, -1),
                        wo[:gw], wo[gw:], vec(ln2_g), vec(ln2_b),
                        ffn2_w13[i].astype(BF16), ffn2_w2[i].astype(BF16),
                        ple_gate_w[i].astype(BF16), ple_proj_w[i].astype(BF16),
                        vec(ln3_g), vec(ln3_b))
    return xs.reshape(bsz, seq, d)
```
